```python
import math
import jax, jax.numpy as jnp
from jax import lax
import numpy as np

D_MODEL = 2048
BATCH = 2
SEQ = 4096
DEPTH = 1
DEC_BATCH = 8
DEC_SEQ = 16
PAST_LEN = 2048

CHUNK = 64
GLA_HEADS = 4
GLA_DK = D_MODEL // 2
GLA_DV = D_MODEL
GLA_DK_HEAD = GLA_DK // GLA_HEADS
GLA_DV_HEAD = GLA_DV // GLA_HEADS
GLA_GATE_RANK = 16
GLA_GATE_NORM = 16.0
CONV_CH = D_MODEL // 2
CONV_WIDTH = 31
PEER_HEADS = 8
PEER_NKEYS = 128
PEER_EXPERTS = PEER_NKEYS * PEER_NKEYS
PEER_DQ = 256
PEER_DQ_HALF = PEER_DQ // 2
PEER_TOPK = 16
PEER_BLOCK = 128
EPS = 1e-6
IN_WIDTHS = (GLA_DK, GLA_DK, GLA_DV, GLA_DV, GLA_GATE_RANK, 2 * CONV_CH, D_MODEL, D_MODEL)
D_IN = GLA_DK * 2 + GLA_DV * 2 + GLA_GATE_RANK + 2 * CONV_CH + 2 * D_MODEL

kernel_name = 'hybrid_gla_conformer_peer_stream_step'


def rms_norm(x, g):
    xf = x.astype(jnp.float32)
    y = xf * lax.rsqrt(jnp.mean(xf * xf, axis=-1, keepdims=True) + EPS)
    return (y * g.astype(jnp.float32)).astype(x.dtype)


def layer_norm(x, g, b):
    xf = x.astype(jnp.float32)
    mu = jnp.mean(xf, axis=-1, keepdims=True)
    xc = xf - mu
    y = xc * lax.rsqrt(jnp.mean(xc * xc, axis=-1, keepdims=True) + EPS)
    return (y * g.astype(jnp.float32) + b.astype(jnp.float32)).astype(x.dtype)


def gla_chunked(q, k, v, log_a, s0):
    B, T, H, Dk = q.shape
    Dv = v.shape[-1]
    L = min(CHUNK, T)
    n = T // L

    def to_chunks(a):
        return jnp.moveaxis(a.astype(jnp.float32).reshape(B, n, L, H, a.shape[-1]), 1, 0)

    mask = jnp.tril(jnp.ones((L, L), dtype=bool))

    def step(S, inp):
        qc, kc, vc, lac = inp
        b = jnp.cumsum(lac, axis=1)
        b_last = b[:, -1]
        q_in = qc * jnp.exp(b)
        k_in = kc * jnp.exp(-b)
        k_out = kc * jnp.exp(b_last[:, None] - b)
        att = jnp.where(mask, jnp.einsum('bthk,bshk->bhts', q_in, k_in), 0.0)
        o = jnp.einsum('bhts,bshv->bthv', att, vc) + jnp.einsum('bthk,bhkv->bthv', q_in, S)
        S = jnp.exp(b_last)[..., None] * S + jnp.einsum('bshk,bshv->bhkv', k_out, vc)
        return S, o

    S, o = lax.scan(step, s0.astype(jnp.float32),
                    (to_chunks(q), to_chunks(k), to_chunks(v), to_chunks(log_a)))
    o = jnp.moveaxis(o, 0, 1).reshape(B, T, H, Dv)
    return o, S


def conformer_conv(zc, buf, w_dw, b_dw, g_ln, b_ln, w_o):
    a, g = jnp.split(zc, 2, axis=-1)
    u = a * jax.nn.sigmoid(g)
    full = jnp.concatenate([buf.astype(u.dtype), u], axis=1)
    y = lax.conv_general_dilated(full, w_dw.astype(full.dtype)[:, None, :], window_strides=(1,),
                                 padding='VALID', dimension_numbers=('NWC', 'WIO', 'NWC'),
                                 feature_group_count=CONV_CH) + b_dw
    new_buf = full[:, -(CONV_WIDTH - 1):]
    y = jax.nn.silu(layer_norm(y, g_ln, b_ln))
    return y @ w_o, new_buf


def peer(h, w_pq, k_sub1, k_sub2, u_exp, v_exp):
    B, T, D = h.shape
    x = h.reshape(B * T, D)
    N = x.shape[0]
    q = (x @ w_pq).astype(jnp.float32).reshape(N, PEER_HEADS, 2, PEER_DQ_HALF)
    s1 = jnp.einsum('nhd,kd->nhk', q[:, :, 0], k_sub1.astype(jnp.float32))
    s2 = jnp.einsum('nhd,kd->nhk', q[:, :, 1], k_sub2.astype(jnp.float32))
    v1, i1 = lax.top_k(s1, PEER_TOPK)
    v2, i2 = lax.top_k(s2, PEER_TOPK)
    cand = (v1[..., :, None] + v2[..., None, :]).reshape(N, PEER_HEADS, PEER_TOPK * PEER_TOPK)
    cidx = (i1[..., :, None] * PEER_NKEYS + i2[..., None, :]).reshape(N, PEER_HEADS, PEER_TOPK * PEER_TOPK)
    sc, pos = lax.top_k(cand, PEER_TOPK)
    idx = jnp.take_along_axis(cidx, pos, axis=-1)
    wts = jax.nn.softmax(sc, axis=-1)
    pad = (-N) % PEER_BLOCK
    xp = jnp.pad(x, ((0, pad), (0, 0)))
    ip = jnp.pad(idx, ((0, pad), (0, 0), (0, 0)))
    wp = jnp.pad(wts, ((0, pad), (0, 0), (0, 0)))
    nb = (N + pad) // PEER_BLOCK

    def block(args):
        xb, ib, wb = args
        ub = jnp.take(u_exp, ib, axis=0)
        act = jax.nn.gelu(jnp.einsum('nhed,nd->nhe', ub, xb).astype(jnp.float32), approximate=False)
        vb = jnp.take(v_exp, ib, axis=0)
        return jnp.einsum('nhe,nhed->nd', wb * act, vb.astype(jnp.float32))

    out = lax.map(block, (xp.reshape(nb, PEER_BLOCK, D),
                          ip.reshape(nb, PEER_BLOCK, PEER_HEADS, PEER_TOPK),
                          wp.reshape(nb, PEER_BLOCK, PEER_HEADS, PEER_TOPK)))
    return out.reshape(nb * PEER_BLOCK, D)[:N].reshape(B, T, D).astype(h.dtype)


def hybrid_layer(x, c, s_gla, buf_conv, w_ada, b_ada, g_pre_mix, g_post_mix, w_in, w_alpha_up, b_alpha,
                 g_gla_out, w_gla_o, w_dw, b_dw, g_conv_ln, b_conv_ln, w_conv_o, w_out,
                 g_pre_ffn, g_post_ffn, w_pq, k_sub1, k_sub2, u_exp, v_exp):
    B, T, _ = x.shape
    mod = jax.nn.silu(c) @ w_ada + b_ada
    sh1, sc1, gt1, sh2, sc2, gt2 = [m[:, None, :] for m in jnp.split(mod, 6, axis=-1)]

    h = rms_norm(x, g_pre_mix) * (1 + sc1) + sh1
    z = h @ w_in
    split_pts = np.cumsum(IN_WIDTHS)[:-1].tolist()
    zq, zk, zv, zr, zlr, zc, zga, zgb = jnp.split(z, split_pts, axis=-1)

    q = zq.reshape(B, T, GLA_HEADS, GLA_DK_HEAD) * (GLA_DK_HEAD ** -0.5)
    k = zk.reshape(B, T, GLA_HEADS, GLA_DK_HEAD)
    v = zv.reshape(B, T, GLA_HEADS, GLA_DV_HEAD)
    log_a = jax.nn.log_sigmoid((zlr @ w_alpha_up + b_alpha).astype(jnp.float32)) / GLA_GATE_NORM
    log_a = log_a.reshape(B, T, GLA_HEADS, GLA_DK_HEAD)
    o, s_new = gla_chunked(q, k, v, log_a, s_gla)
    o = rms_norm(o, g_gla_out).reshape(B, T, GLA_DV) * jax.nn.silu(zr.astype(jnp.float32))
    y_a = o.astype(x.dtype) @ w_gla_o

    y_b, buf_new = conformer_conv(zc, buf_conv, w_dw, b_dw, g_conv_ln, b_conv_ln, w_conv_o)

    mix = (jax.nn.sigmoid(zga) * y_a + jax.nn.sigmoid(zgb) * y_b) @ w_out
    x = x + gt1 * rms_norm(mix, g_post_mix)

    h2 = rms_norm(x, g_pre_ffn) * (1 + sc2) + sh2
    f = peer(h2, w_pq, k_sub1, k_sub2, u_exp, v_exp)
    x = x + gt2 * rms_norm(f, g_post_ffn)
    return x, s_new, buf_new


def setup_inputs(seed: int = 0) -> dict:
    key = jax.random.key(seed)
    ks = jax.random.split(key, 32)
    D = D_MODEL
    L = DEPTH

    def nrm(k, shape, s):
        return jax.random.normal(k, shape, jnp.float32) * s

    return {
        'x_prompt': nrm(ks[0], (BATCH, SEQ, D), 1.0),
        'x_sample': nrm(ks[1], (DEC_BATCH, DEC_SEQ, D), 1.0),
        'state_gla': nrm(ks[2], (L, DEC_BATCH, GLA_HEADS, GLA_DK_HEAD, GLA_DV_HEAD), 0.1),
        'cache_conv': nrm(ks[3], (L, DEC_BATCH, CONV_WIDTH - 1, CONV_CH), 0.5),
        'c_prompt': nrm(ks[4], (BATCH, D), 1.0),
        'c_sample': nrm(ks[5], (DEC_BATCH, D), 1.0),
        'w_ada': nrm(ks[6], (L, D, 6 * D), 0.5 * D ** -0.5),
        'b_ada': nrm(ks[7], (L, 6 * D), 0.02),
        'g_pre_mix': 1.0 + nrm(ks[8], (L, D), 0.02),
        'g_post_mix': 1.0 + nrm(ks[9], (L, D), 0.02),
        'w_in': nrm(ks[10], (L, D, D_IN), D ** -0.5),
        'w_alpha_up': nrm(ks[11], (L, GLA_GATE_RANK, GLA_DK), GLA_GATE_RANK ** -0.5),
        'b_alpha': nrm(ks[12], (L, GLA_DK), 0.1),
        'g_gla_out': 1.0 + nrm(ks[13], (L, GLA_DV_HEAD), 0.02),
        'w_gla_o': nrm(ks[14], (L, GLA_DV, D), GLA_DV ** -0.5),
        'w_dw': nrm(ks[15], (L, CONV_WIDTH, CONV_CH), CONV_WIDTH ** -0.5),
        'b_dw': nrm(ks[16], (L, CONV_CH), 0.02),
        'g_conv_ln': 1.0 + nrm(ks[17], (L, CONV_CH), 0.02),
        'b_conv_ln': nrm(ks[18], (L, CONV_CH), 0.02),
        'w_conv_o': nrm(ks[19], (L, CONV_CH, D), CONV_CH ** -0.5),
        'w_out': nrm(ks[20], (L, D, D), D ** -0.5),
        'g_pre_ffn': 1.0 + nrm(ks[21], (L, D), 0.02),
        'g_post_ffn': 1.0 + nrm(ks[22], (L, D), 0.02),
        'w_pq': nrm(ks[23], (L, D, PEER_HEADS * PEER_DQ), D ** -0.5),
        'k_sub1': nrm(ks[24], (L, PEER_NKEYS, PEER_DQ_HALF), PEER_DQ_HALF ** -0.5),
        'k_sub2': nrm(ks[25], (L, PEER_NKEYS, PEER_DQ_HALF), PEER_DQ_HALF ** -0.5),
        'u_exp': nrm(ks[26], (L, PEER_EXPERTS, D), D ** -0.5),
        'v_exp': nrm(ks[27], (L, PEER_EXPERTS, D), 0.25),
    }


def reference(x_prompt, x_sample, state_gla, cache_conv, c_prompt, c_sample, w_ada, b_ada, g_pre_mix,
              g_post_mix, w_in, w_alpha_up, b_alpha, g_gla_out, w_gla_o, w_dw, b_dw, g_conv_ln, b_conv_ln,
              w_conv_o, w_out, g_pre_ffn, g_post_ffn, w_pq, k_sub1, k_sub2, u_exp, v_exp):
    params = (w_ada, b_ada, g_pre_mix, g_post_mix, w_in, w_alpha_up, b_alpha, g_gla_out, w_gla_o, w_dw,
              b_dw, g_conv_ln, b_conv_ln, w_conv_o, w_out, g_pre_ffn, g_post_ffn, w_pq, k_sub1, k_sub2,
              u_exp, v_exp)
    xp, xs = x_prompt, x_sample
    bp = xp.shape[0]
    sg_p, cb_p, sg_s, cb_s = [], [], [], []
    for l in range(DEPTH):
        pl = [p[l] for p in params]
        s0 = jnp.zeros((bp, GLA_HEADS, GLA_DK_HEAD, GLA_DV_HEAD), jnp.float32)
        buf0 = jnp.zeros((bp, CONV_WIDTH - 1, CONV_CH), xp.dtype)
        xp, s_p, b_p = hybrid_layer(xp, c_prompt, s0, buf0, *pl)
        xs, s_s, b_s = hybrid_layer(xs, c_sample, state_gla[l], cache_conv[l], *pl)
        sg_p.append(s_p)
        cb_p.append(b_p)
        sg_s.append(s_s)
        cb_s.append(b_s)
    return (xp, xs, jnp.stack(sg_p), jnp.stack(cb_p), jnp.stack(sg_s), jnp.stack(cb_s))
```

```python
import functools

import jax
import jax.numpy as jnp
from jax import lax
from jax.experimental import pallas as pl
from jax.experimental.pallas import tpu as pltpu

F32 = jnp.float32
BF16 = jnp.bfloat16

CHUNK = 64
GLA_HEADS = 4
GLA_GATE_RANK = 16
GLA_GATE_NORM = 16.0
CONV_WIDTH = 31
PEER_HEADS = 8
PEER_NKEYS = 128
PEER_TOPK = 16
EPS = 1e-6

LANES = 128
SUBLANES = 8
CONV_HALO = 32
VMEM_LIMIT = 52 * 1024 * 1024


def _params(*sem):
    return pltpu.CompilerParams(dimension_semantics=sem, vmem_limit_bytes=VMEM_LIMIT)


def _const_spec(shape):
    nd = len(shape)
    return pl.BlockSpec(shape, lambda *_: (0,) * nd, pipeline_mode=pl.Buffered(1))


def _rms(x):
    return x * lax.rsqrt(jnp.mean(x * x, axis=-1, keepdims=True) + EPS)


def _sigmoid(x):
    return 1.0 / (1.0 + jnp.exp(-x))


def _silu(x):
    return x * _sigmoid(x)


def _ada_kernel(c_ref, w_ref, b_ref, o_ref):
    c = c_ref[...]
    o_ref[...] = jnp.dot(_silu(c), w_ref[...], preferred_element_type=F32,
                         precision=lax.Precision.HIGHEST) + b_ref[...]


def _ada(c, w_ada, b_ada):
    g, d = c.shape
    n = w_ada.shape[1]
    tc = 1024
    return pl.pallas_call(
        _ada_kernel,
        grid=(n // tc,),
        in_specs=[pl.BlockSpec((g, d), lambda j: (0, 0)),
                  pl.BlockSpec((d, tc), lambda j: (0, j)),
                  pl.BlockSpec((1, tc), lambda j: (0, j))],
        out_specs=pl.BlockSpec((g, tc), lambda j: (0, j)),
        out_shape=jax.ShapeDtypeStruct((g, n), F32),
        compiler_params=_params("arbitrary"),
        name="ada",
    )(c, w_ada, b_ada.reshape(1, n))


def _inproj_kernel(x_ref, sc_ref, sh_ref, g_ref, w_ref, wlr_ref, z_ref, zlr_ref, h_scr):
    @pl.when(pl.program_id(2) == 0)
    def _():
        h = _rms(x_ref[0]) * g_ref[...]
        h = (h * (1.0 + sc_ref[0]) + sh_ref[0]).astype(BF16)
        h_scr[...] = h
        zlr_ref[0] = jnp.dot(h, wlr_ref[...], preferred_element_type=F32)

    z_ref[0] = jnp.dot(h_scr[...], w_ref[...], preferred_element_type=F32)


def _mod_spec(m, tt):
    d = m.shape[-1]
    if m.shape[1] == 1:
        return pl.BlockSpec((1, 1, d), lambda b, i, *_: (b, 0, 0))
    return pl.BlockSpec((1, tt, d), lambda b, i, *_: (b, i, 0))


def _inproj(x, sc, sh, g, w_main, w_lr, tt):
    bsz, t, d = x.shape
    n = w_main.shape[1]
    tc = 1024
    return pl.pallas_call(
        _inproj_kernel,
        grid=(bsz, t // tt, n // tc),
        in_specs=[pl.BlockSpec((1, tt, d), lambda b, i, j: (b, i, 0)),
                  _mod_spec(sc, tt), _mod_spec(sh, tt),
                  pl.BlockSpec((1, d), lambda b, i, j: (0, 0)),
                  pl.BlockSpec((d, tc), lambda b, i, j: (0, j)),
                  pl.BlockSpec((d, LANES), lambda b, i, j: (0, 0))],
        out_specs=[pl.BlockSpec((1, tt, tc), lambda b, i, j: (b, i, j)),
                   pl.BlockSpec((1, tt, LANES), lambda b, i, j: (b, i, 0))],
        out_shape=[jax.ShapeDtypeStruct((bsz, t, n), F32),
                   jax.ShapeDtypeStruct((bsz, t, LANES), F32)],
        scratch_shapes=[pltpu.VMEM((tt, d), BF16)],
        compiler_params=_params("parallel", "parallel", "arbitrary"),
        name="inproj",
    )(x, sc, sh, g, w_main, w_lr)


def _gla_kernel(q_ref, k_ref, v_ref, r_ref, lr_ref, wau_ref, ba_ref, gg_ref, s0_ref,
                o_ref, s_ref, st_scr, *, dk, dv):
    c = pl.program_id(1)
    rows = q_ref.shape[1]

    @pl.when(c == 0)
    def _():
        for h in range(GLA_HEADS):
            st_scr[h] = s0_ref[0, h].T

    pre = jnp.dot(lr_ref[0], wau_ref[...], preferred_element_type=F32,
                  precision=lax.Precision.HIGHEST) + ba_ref[...]
    log_a = (jnp.minimum(pre, 0.0) - jnp.log1p(jnp.exp(-jnp.abs(pre)))) * (1.0 / GLA_GATE_NORM)
    row = lax.broadcasted_iota(jnp.int32, log_a.shape, 0)
    b = log_a
    shift = 1
    while shift < rows:
        b = b + jnp.where(row >= shift, pltpu.roll(b, shift, axis=0), 0.0)
        shift *= 2
    b_last = b[rows - 1:rows, :]
    e_pos = jnp.exp(b)
    q_in = q_ref[0] * (dk ** -0.5) * e_pos
    kk = k_ref[0]
    k_in = kk * jnp.exp(-b)
    k_out = kk * jnp.exp(b_last - b)
    e_last = jnp.exp(b_last)
    causal = (lax.broadcasted_iota(jnp.int32, (rows, rows), 0)
              >= lax.broadcasted_iota(jnp.int32, (rows, rows), 1))
    gg = gg_ref[...]
    for h in range(GLA_HEADS):
        ks = slice(h * dk, (h + 1) * dk)
        vs = slice(h * dv, (h + 1) * dv)
        qh = q_in[:, ks].astype(BF16)
        vh = v_ref[0, :, vs].astype(BF16)
        st = st_scr[h]
        att = lax.dot_general(qh, k_in[:, ks].astype(BF16), (((1,), (1,)), ((), ())),
                              preferred_element_type=F32)
        att = jnp.where(causal, att, 0.0).astype(BF16)
        o = jnp.dot(att, vh, preferred_element_type=F32)
        o = o + lax.dot_general(qh, st.astype(BF16), (((1,), (1,)), ((), ())),
                                preferred_element_type=F32)
        st_scr[h] = st * e_last[:, ks] + lax.dot_general(
            vh, k_out[:, ks].astype(BF16), (((0,), (0,)), ((), ())), preferred_element_type=F32)
        o = _rms(o) * gg * _silu(r_ref[0, :, vs])
        o_ref[0, :, vs] = o.astype(o_ref.dtype)

    @pl.when(c == pl.num_programs(1) - 1)
    def _():
        for h in range(GLA_HEADS):
            s_ref[0, h] = st_scr[h].T


def _gla(z, zlr, w_au, b_alpha, g_gla_out, s0):
    bsz, t, _ = z.shape
    _, nh, dk, dv = s0.shape
    rows = min(CHUNK, t)
    kern = functools.partial(_gla_kernel, dk=dk, dv=dv)
    return pl.pallas_call(
        kern,
        grid=(bsz, t // rows),
        in_specs=[pl.BlockSpec((1, rows, nh * dk), lambda b, c: (b, c, 0)),
                  pl.BlockSpec((1, rows, nh * dk), lambda b, c: (b, c, 1)),
                  pl.BlockSpec((1, rows, nh * dv), lambda b, c: (b, c, 1)),
                  pl.BlockSpec((1, rows, nh * dv), lambda b, c: (b, c, 2)),
                  pl.BlockSpec((1, rows, LANES), lambda b, c: (b, c, 0)),
                  pl.BlockSpec((LANES, nh * dk), lambda b, c: (0, 0)),
                  pl.BlockSpec((1, nh * dk), lambda b, c: (0, 0)),
                  pl.BlockSpec((1, dv), lambda b, c: (0, 0)),
                  pl.BlockSpec((1, nh, dk, dv), lambda b, c: (b, 0, 0, 0))],
        out_specs=[pl.BlockSpec((1, rows, nh * dv), lambda b, c: (b, c, 0)),
                   pl.BlockSpec((1, nh, dk, dv), lambda b, c: (b, 0, 0, 0))],
        out_shape=[jax.ShapeDtypeStruct((bsz, t, nh * dv), BF16),
                   jax.ShapeDtypeStruct((bsz, nh, dk, dv), F32)],
        scratch_shapes=[pltpu.VMEM((nh, dv, dk), F32)],
        compiler_params=_params("parallel", "arbitrary"),
        name="gla",
    )(z, z, z, z, zlr, w_au, b_alpha, g_gla_out, s0)


def _conv_kernel(a_ref, g_ref, cache_ref, wdw_ref, bdw_ref, gln_ref, bln_ref,
                 y_ref, newc_ref, win_scr):
    i = pl.program_id(1)
    tt = a_ref.shape[1]
    hist = CONV_WIDTH - 1
    pad = CONV_HALO - hist

    @pl.when(i == 0)
    def _():
        win_scr[0:pad, :] = jnp.zeros((pad, win_scr.shape[1]), F32)
        win_scr[pad:CONV_HALO, :] = cache_ref[0]

    win_scr[CONV_HALO:CONV_HALO + tt, :] = a_ref[0] * _sigmoid(g_ref[0])
    bdw = bdw_ref[...]
    gln = gln_ref[...]
    bln = bln_ref[...]
    for r0 in range(0, tt, SUBLANES):
        acc = jnp.zeros((SUBLANES, win_scr.shape[1]), F32) + bdw
        for w in range(CONV_WIDTH):
            acc = acc + win_scr[r0 + pad + w:r0 + pad + w + SUBLANES, :] * wdw_ref[w:w + 1, :]
        mu = jnp.mean(acc, axis=-1, keepdims=True)
        xc = acc - mu
        yn = xc * lax.rsqrt(jnp.mean(xc * xc, axis=-1, keepdims=True) + EPS) * gln + bln
        y_ref[0, r0:r0 + SUBLANES, :] = _silu(yn).astype(y_ref.dtype)

    @pl.when(i == pl.num_programs(1) - 1)
    def _():
        newc_ref[0] = win_scr[tt + pad:tt + CONV_HALO, :]

    tail = win_scr[tt:tt + CONV_HALO, :]
    win_scr[0:CONV_HALO, :] = tail


def _conv(z, cache, w_dw, b_dw, g_ln, b_ln, tt):
    bsz, t, _ = z.shape
    _, hist, ch = cache.shape
    return pl.pallas_call(
        _conv_kernel,
        grid=(bsz, t // tt),
        in_specs=[pl.BlockSpec((1, tt, ch), lambda b, i: (b, i, 6)),
                  pl.BlockSpec((1, tt, ch), lambda b, i: (b, i, 7)),
                  pl.BlockSpec((1, hist, ch), lambda b, i: (b, 0, 0)),
                  pl.BlockSpec((CONV_WIDTH, ch), lambda b, i: (0, 0)),
                  pl.BlockSpec((1, ch), lambda b, i: (0, 0)),
                  pl.BlockSpec((1, ch), lambda b, i: (0, 0)),
                  pl.BlockSpec((1, ch), lambda b, i: (0, 0))],
        out_specs=[pl.BlockSpec((1, tt, ch), lambda b, i: (b, i, 0)),
                   pl.BlockSpec((1, hist, ch), lambda b, i: (b, 0, 0))],
        out_shape=[jax.ShapeDtypeStruct((bsz, t, ch), BF16),
                   jax.ShapeDtypeStruct((bsz, hist, ch), F32)],
        scratch_shapes=[pltpu.VMEM((tt + CONV_HALO, ch), F32)],
        compiler_params=_params("parallel", "arbitrary"),
        name="conv",
    )(z, z, cache, w_dw, b_dw, g_ln, b_ln)


def _merge_kernel(oa_ref, cb_ref, ga_ref, gb_ref, x_ref, gt1_ref, sc2_ref, sh2_ref,
                  gpost_ref, gpre_ref, wgo_ref, wco_ref, wout_ref, x1_ref, h2_ref):
    ya = jnp.dot(oa_ref[0], wgo_ref[...], preferred_element_type=F32)
    yb = jnp.dot(cb_ref[0], wco_ref[...], preferred_element_type=F32)
    m = _sigmoid(ga_ref[0]) * ya + _sigmoid(gb_ref[0]) * yb
    mix = jnp.dot(m.astype(BF16), wout_ref[...], preferred_element_type=F32)
    x1 = x_ref[0] + gt1_ref[0] * (_rms(mix) * gpost_ref[...])
    x1_ref[0] = x1
    h2 = _rms(x1) * gpre_ref[...]
    h2_ref[0] = (h2 * (1.0 + sc2_ref[0]) + sh2_ref[0]).astype(h2_ref.dtype)


def _merge(oa, cb, z, x, gt1, sc2, sh2, g_post, g_pre, w_go, w_co, w_out, tt):
    bsz, t, d = x.shape
    ch = cb.shape[-1]
    tok = lambda w: pl.BlockSpec((1, tt, w), lambda b, i: (b, i, 0))
    return pl.pallas_call(
        _merge_kernel,
        grid=(bsz, t // tt),
        in_specs=[tok(d), tok(ch),
                  pl.BlockSpec((1, tt, d), lambda b, i: (b, i, 4)),
                  pl.BlockSpec((1, tt, d), lambda b, i: (b, i, 5)),
                  tok(d), _mod_spec(gt1, tt), _mod_spec(sc2, tt), _mod_spec(sh2, tt),
                  _const_spec((1, d)), _const_spec((1, d)),
                  _const_spec(w_go.shape), _const_spec(w_co.shape), _const_spec(w_out.shape)],
        out_specs=[tok(d), tok(d)],
        out_shape=[jax.ShapeDtypeStruct((bsz, t, d), F32),
                   jax.ShapeDtypeStruct((bsz, t, d), BF16)],
        compiler_params=_params("parallel", "parallel"),
        name="merge",
    )(oa, cb, z, z, x, gt1, sc2, sh2, g_post, g_pre, w_go, w_co, w_out)


def _keyfold_kernel(ks_ref, wq_ref, o_ref):
    o_ref[...] = lax.dot_general(ks_ref[0], wq_ref[...], (((1,), (1,)), ((), ())),
                                 preferred_element_type=F32,
                                 precision=lax.Precision.HIGHEST).astype(o_ref.dtype)


def _keyfold(k_sub, w_pq):
    d = w_pq.shape[0]
    _, nk, dq = k_sub.shape
    nblk = w_pq.shape[1] // dq
    return pl.pallas_call(
        _keyfold_kernel,
        grid=(nblk,),
        in_specs=[pl.BlockSpec((1, nk, dq), lambda j: (j % 2, 0, 0)),
                  pl.BlockSpec((d, dq), lambda j: (0, j))],
        out_specs=pl.BlockSpec((nk, d), lambda j: (j, 0)),
        out_shape=jax.ShapeDtypeStruct((nblk * nk, d), BF16),
        compiler_params=_params("arbitrary"),
        name="keyfold",
    )(k_sub, w_pq)


def _top16(s):
    rows = s.shape[0]
    rid = lax.broadcasted_iota(jnp.int32, s.shape, 0).astype(F32)
    vals, idxs = [], []
    for _ in range(PEER_TOPK):
        m = jnp.max(s, axis=0, keepdims=True)
        am = jnp.min(jnp.where(s == m, rid, float(rows)), axis=0, keepdims=True)
        vals.append(m)
        idxs.append(am)
        s = jnp.where(rid == am, -jnp.inf, s)
    return vals, idxs


def _stack_rows(rows_list, n):
    k = len(rows_list)
    rid = lax.broadcasted_iota(jnp.int32, (k, n), 0)
    out = jnp.zeros((k, n), F32)
    for r, v in enumerate(rows_list):
        out = jnp.where(rid == r, v, out)
    return out


def _topk_kernel(h_ref, tab_ref, w_ref, a_ref, b_ref, w_scr, a_scr, b_scr):
    hd = pl.program_id(1)
    tn = h_ref.shape[0]
    nk = PEER_NKEYS
    s = lax.dot_general(tab_ref[...], h_ref[...], (((1,), (1,)), ((), ())),
                        preferred_element_type=F32)
    v1, i1 = _top16(s[0:nk])
    v2, i2 = _top16(s[nk:2 * nk])
    v2a = _stack_rows(v2, tn)
    i2a = _stack_rows(i2, tn)
    v1a = _stack_rows(v1, tn)
    i1a = _stack_rows(i1, tn)
    sub = lax.broadcasted_iota(jnp.int32, (SUBLANES, tn), 0)
    cand, cidx = [], []
    for a in range(SUBLANES):
        lim = PEER_TOPK // (a + 1)
        for b0 in range(0, lim, SUBLANES):
            cv = v1[a] + v2a[b0:b0 + SUBLANES]
            ci = i1[a] * float(nk) + i2a[b0:b0 + SUBLANES]
            if lim - b0 < SUBLANES:
                cv = jnp.where(sub < lim - b0, cv, -jnp.inf)
            cand.append(cv)
            cidx.append(ci)
    cand.append(v1a[SUBLANES:] + v2[0])
    cidx.append(i1a[SUBLANES:] * float(nk) + i2[0])
    cand = jnp.concatenate(cand, axis=0)
    cidx = jnp.concatenate(cidx, axis=0)
    rid = lax.broadcasted_iota(jnp.int32, cand.shape, 0).astype(F32)
    sc, ids = [], []
    for _ in range(PEER_TOPK):
        m = jnp.max(cand, axis=0, keepdims=True)
        am = jnp.min(jnp.where(cand == m, rid, float(cand.shape[0])), axis=0, keepdims=True)
        hit = rid == am
        sc.append(m)
        ids.append(jnp.sum(jnp.where(hit, cidx, 0.0), axis=0, keepdims=True))
        cand = jnp.where(hit, -jnp.inf, cand)
    sca = _stack_rows(sc, tn)
    ida = _stack_rows(ids, tn)
    e = jnp.exp(sca - sc[0])
    wts = e / jnp.sum(e, axis=0, keepdims=True)
    ia = jnp.floor(ida * (1.0 / nk))
    ib = ida - ia * float(nk)
    r0 = pl.multiple_of(hd * PEER_TOPK, PEER_TOPK)
    w_scr[pl.ds(r0, PEER_TOPK), :] = wts
    a_scr[pl.ds(r0, PEER_TOPK), :] = ia
    b_scr[pl.ds(r0, PEER_TOPK), :] = ib

    @pl.when(hd == pl.num_programs(1) - 1)
    def _():
        w_ref[...] = w_scr[...].T
        a_ref[...] = a_scr[...].T
        b_ref[...] = b_scr[...].T


def _topk(h2, table, tn):
    n, d = h2.shape
    nsel = PEER_HEADS * PEER_TOPK
    out = jax.ShapeDtypeStruct((n, nsel), F32)
    ospec = pl.BlockSpec((tn, nsel), lambda i, h: (i, 0))
    return pl.pallas_call(
        _topk_kernel,
        grid=(n // tn, PEER_HEADS),
        in_specs=[pl.BlockSpec((tn, d), lambda i, h: (i, 0)),
                  pl.BlockSpec((2 * PEER_NKEYS, d), lambda i, h: (h, 0))],
        out_specs=[ospec, ospec, ospec],
        out_shape=[out, out, out],
        scratch_shapes=[pltpu.VMEM((nsel, tn), F32)] * 3,
        compiler_params=_params("parallel", "arbitrary"),
        name="topk",
    )(h2, table)


def _act_kernel(h_ref, u_ref, w_ref, a_ref, b_ref, coef_ref, sel_scr):
    k = pl.program_id(1)
    te = u_ref.shape[0]

    @pl.when(k == 0)
    def _():
        sel_scr[...] = jnp.zeros(sel_scr.shape, F32)

    s = lax.dot_general(h_ref[...], u_ref[...], (((1,), (1,)), ((), ())),
                        preferred_element_type=F32)
    ia = a_ref[...]
    ib = b_ref[...].astype(jnp.int32)
    sel = sel_scr[...]
    base = (k * (te // LANES)).astype(F32)
    for j in range(te // LANES):
        g = jnp.take_along_axis(s[:, j * LANES:(j + 1) * LANES], ib, axis=1)
        sel = jnp.where(ia == base + float(j), g, sel)
    sel_scr[...] = sel

    @pl.when(k == pl.num_programs(1) - 1)
    def _():
        act = 0.5 * sel * (1.0 + lax.erf(sel * (2.0 ** -0.5)))
        coef_ref[...] = w_ref[...] * act


def _expert_act(h2, u_bf, wts, ia, ib, tn, te):
    n, d = h2.shape
    ne = u_bf.shape[0]
    nsel = wts.shape[1]
    sspec = pl.BlockSpec((tn, nsel), lambda i, k: (i, 0))
    return pl.pallas_call(
        _act_kernel,
        grid=(n // tn, ne // te),
        in_specs=[pl.BlockSpec((tn, d), lambda i, k: (i, 0)),
                  pl.BlockSpec((te, d), lambda i, k: (k, 0)),
                  sspec, sspec, sspec],
        out_specs=sspec,
        out_shape=jax.ShapeDtypeStruct((n, nsel), F32),
        scratch_shapes=[pltpu.VMEM((tn, nsel), F32)],
        compiler_params=_params("parallel", "arbitrary"),
        name="expert_act",
    )(h2, u_bf, wts, ia, ib)


def _mixmat_kernel(coef_ref, a_ref, b_ref, p_ref, p3_scr):
    tn = coef_ref.shape[0]
    nk = PEER_NKEYS
    kid = lax.broadcasted_iota(jnp.int32, (nk, coef_ref.shape[1]), 0).astype(F32)

    def body(t, carry):
        c = coef_ref[pl.ds(t, 1), :]
        ia = a_ref[pl.ds(t, 1), :]
        ib = b_ref[pl.ds(t, 1), :]
        lhs = jnp.where(ia == kid, c, 0.0).astype(BF16)
        rhs = jnp.where(ib == kid, 1.0, 0.0).astype(BF16)
        p3_scr[t] = lax.dot_general(lhs, rhs, (((1,), (1,)), ((), ())),
                                    preferred_element_type=F32)
        return carry

    lax.fori_loop(0, tn, body, 0)
    for j in range(nk):
        p_ref[:, j * nk:(j + 1) * nk] = p3_scr[:, j, :].astype(p_ref.dtype)


def _mixmat(coef, ia, ib, tn):
    n, nsel = coef.shape
    ne = PEER_NKEYS * PEER_NKEYS
    sspec = pl.BlockSpec((tn, nsel), lambda i: (i, 0))
    return pl.pallas_call(
        _mixmat_kernel,
        grid=(n // tn,),
        in_specs=[sspec, sspec, sspec],
        out_specs=pl.BlockSpec((tn, ne), lambda i: (i, 0)),
        out_shape=jax.ShapeDtypeStruct((n, ne), BF16),
        scratch_shapes=[pltpu.VMEM((tn, PEER_NKEYS, PEER_NKEYS), F32)],
        compiler_params=_params("parallel"),
        name="mixmat",
    )(coef, ia, ib)


def _value_kernel(p_ref, v_ref, x1_ref, gt2_ref, g_ref, y_ref, acc_scr):
    k = pl.program_id(2)

    @pl.when(k == 0)
    def _():
        acc_scr[...] = jnp.zeros(acc_scr.shape, F32)

    acc_scr[...] += jnp.dot(p_ref[0], v_ref[...], preferred_element_type=F32)

    @pl.when(k == pl.num_programs(2) - 1)
    def _():
        y_ref[0] = x1_ref[0] + gt2_ref[0] * (_rms(acc_scr[...]) * g_ref[...])


def _value(p, v_bf, x1, gt2, g_post, tt, tk):
    bsz, t, d = x1.shape
    ne = v_bf.shape[0]
    return pl.pallas_call(
        _value_kernel,
        grid=(bsz, t // tt, ne // tk),
        in_specs=[pl.BlockSpec((1, tt, tk), lambda b, i, k: (b, i, k)),
                  pl.BlockSpec((tk, d), lambda b, i, k: (k, 0)),
                  pl.BlockSpec((1, tt, d), lambda b, i, k: (b, i, 0)),
                  _mod_spec(gt2, tt),
                  pl.BlockSpec((1, d), lambda b, i, k: (0, 0))],
        out_specs=pl.BlockSpec((1, tt, d), lambda b, i, k: (b, i, 0)),
        out_shape=jax.ShapeDtypeStruct((bsz, t, d), F32),
        scratch_shapes=[pltpu.VMEM((tt, d), F32)],
        compiler_params=_params("parallel", "parallel", "arbitrary"),
        name="value",
    )(p, v_bf, x1, gt2, g_post)


def _layer(x, mods, s0, cache, wts, *, rec_shape, tt):
    sh1, sc1, gt1, sh2, sc2, gt2 = mods
    bsz, t, d = x.shape
    z, zlr = _inproj(x, sc1, sh1, wts["g_pre_mix"], wts["w_main"], wts["w_lr"], tt)
    rs, rt = rec_shape
    zr = z.reshape(rs, rt, z.shape[-1])
    oa, s_new = _gla(zr, zlr.reshape(rs, rt, LANES), wts["w_au"], wts["b_alpha"],
                     wts["g_gla_out"], s0)
    cb, c_new = _conv(zr, cache, wts["w_dw"], wts["b_dw"], wts["g_conv_ln"], wts["b_conv_ln"],
                      min(rt, 256))
    x1, h2 = _merge(oa.reshape(bsz, t, -1), cb.reshape(bsz, t, -1), z, x, gt1, sc2, sh2,
                    wts["g_post_mix"], wts["g_pre_ffn"], wts["w_gla_o"], wts["w_conv_o"],
                    wts["w_out"], min(t, 256))
    n = bsz * t
    h2f = h2.reshape(n, d)
    pw, ia, ib = _topk(h2f, wts["key_table"], min(n, 256))
    coef = _expert_act(h2f, wts["u_bf"], pw, ia, ib, min(n, 512), 1024)
    p = _mixmat(coef, ia, ib, 128)
    y = _value(p.reshape(bsz, t, -1), wts["v_bf"], x1, gt2, wts["g_post_ffn"], min(t, 512), 1024)
    return y, s_new, c_new


def kernel(x_prompt, x_sample, state_gla, cache_conv, c_prompt, c_sample, w_ada, b_ada, g_pre_mix,
           g_post_mix, w_in, w_alpha_up, b_alpha, g_gla_out, w_gla_o, w_dw, b_dw, g_conv_ln,
           b_conv_ln, w_conv_o, w_out, g_pre_ffn, g_post_ffn, w_pq, k_sub1, k_sub2, u_exp, v_exp):
    depth = w_ada.shape[0]
    bp, tp, d = x_prompt.shape
    bs, ts, _ = x_sample.shape
    nh, dk, dv = state_gla.shape[2:]
    xp, xs = x_prompt, x_sample.reshape(1, bs * ts, d)
    outs = [[], [], [], []]
    for l in range(depth):
        lr0 = 2 * nh * dk + 2 * nh * dv
        wl = w_in[l]
        row = lambda a: a[l].reshape(1, -1)
        wts = dict(
            w_main=jnp.concatenate([wl[:, :lr0], wl[:, lr0 + GLA_GATE_RANK:]], axis=1).astype(BF16),
            w_lr=jnp.pad(wl[:, lr0:lr0 + GLA_GATE_RANK],
                         ((0, 0), (0, LANES - GLA_GATE_RANK))).astype(BF16),
            w_au=jnp.pad(w_alpha_up[l], ((0, LANES - GLA_GATE_RANK), (0, 0))),
            b_alpha=row(b_alpha), g_gla_out=row(g_gla_out), g_pre_mix=row(g_pre_mix),
            g_post_mix=row(g_post_mix), g_pre_ffn=row(g_pre_ffn), g_post_ffn=row(g_post_ffn),
            w_dw=w_dw[l], b_dw=row(b_dw), g_conv_ln=row(g_conv_ln), b_conv_ln=row(b_conv_ln),
            w_gla_o=w_gla_o[l].astype(BF16), w_conv_o=w_conv_o[l].astype(BF16),
            w_out=w_out[l].astype(BF16),
            key_table=_keyfold(jnp.stack([k_sub1[l], k_sub2[l]]), w_pq[l]),
            u_bf=u_exp[l].astype(BF16), v_bf=v_exp[l].astype(BF16),
        )
        c_all = jnp.concatenate(
            [c_prompt, c_sample, jnp.zeros((-(bp + bs) % SUBLANES, d), F32)], axis=0)
        mod = _ada(c_all, w_ada[l], b_ada[l])
        mods_p = [m.reshape(bp, 1, d) for m in jnp.split(mod[:bp], 6, axis=-1)]
        mods_s = [jnp.repeat(m, ts, axis=0).reshape(1, bs * ts, d)
                  for m in jnp.split(mod[bp:bp + bs], 6, axis=-1)]
        s0_p = jnp.zeros((bp, nh, dk, dv), F32)
        cache0_p = jnp.zeros((bp, CONV_WIDTH - 1, cache_conv.shape[-1]), F32)
        xp, s_p, cb_p = _layer(xp, mods_p, s0_p, cache0_p, wts, rec_shape=(bp, tp), tt=512)
        xs, s_s, cb_s = _layer(xs, mods_s, state_gla[l], cache_conv[l], wts,
                               rec_shape=(bs, ts), tt=bs * ts)
        for o, v in zip(outs, (s_p, cb_p, s_s, cb_s)):
            o.append(v)
    return (xp, xs.reshape(bs, ts, d), jnp.stack(outs[0]), jnp.stack(outs[1]),
            jnp.stack(outs[2]), jnp.stack(outs[3]))
```

```python
import functools

import jax
import jax.numpy as jnp
from jax import lax
from jax.experimental import pallas as pl
from jax.experimental.pallas import tpu as pltpu

F32 = jnp.float32
BF16 = jnp.bfloat16

CHUNK = 64
GLA_HEADS = 4
GLA_GATE_RANK = 16
GLA_GATE_NORM = 16.0
CONV_WIDTH = 31
PEER_HEADS = 8
PEER_NKEYS = 128
PEER_TOPK = 16
EPS = 1e-6

LANES = 128
SUBLANES = 8
MXU_WIDTH = 256
CONV_HALO = 32
VMEM_LIMIT = 52 * 1024 * 1024


def _params(*sem):
    return pltpu.CompilerParams(dimension_semantics=sem, vmem_limit_bytes=VMEM_LIMIT)


def _const_spec(shape):
    nd = len(shape)
    return pl.BlockSpec(shape, lambda *_: (0,) * nd, pipeline_mode=pl.Buffered(1))


def _rms(x):
    return x * lax.rsqrt(jnp.mean(x * x, axis=-1, keepdims=True) + EPS)


def _sigmoid(x):
    return 1.0 / (1.0 + jnp.exp(-x))


def _silu(x):
    return x * _sigmoid(x)


def _ada_kernel(c_ref, w_ref, b_ref, o_ref):
    c = c_ref[...]
    o_ref[...] = jnp.dot(_silu(c), w_ref[...], preferred_element_type=F32,
                         precision=lax.Precision.HIGHEST) + b_ref[...]


def _ada(c, w_ada, b_ada):
    g, d = c.shape
    n = w_ada.shape[1]
    tc = 1024
    return pl.pallas_call(
        _ada_kernel,
        grid=(n // tc,),
        in_specs=[pl.BlockSpec((g, d), lambda j: (0, 0)),
                  pl.BlockSpec((d, tc), lambda j: (0, j)),
                  pl.BlockSpec((1, tc), lambda j: (0, j))],
        out_specs=pl.BlockSpec((g, tc), lambda j: (0, j)),
        out_shape=jax.ShapeDtypeStruct((g, n), F32),
        compiler_params=_params("arbitrary"),
        name="ada",
    )(c, w_ada, b_ada.reshape(1, n))


def _inproj_kernel(x_ref, sc_ref, sh_ref, g_ref, w_ref, wlr_ref, z_ref, zlr_ref, h_scr):
    @pl.when(pl.program_id(2) == 0)
    def _():
        h = _rms(x_ref[0]) * g_ref[...]
        h = (h * (1.0 + sc_ref[0]) + sh_ref[0]).astype(BF16)
        h_scr[...] = h
        zlr_ref[0] = lax.dot_general(h, wlr_ref[...], (((1,), (1,)), ((), ())),
                                     preferred_element_type=F32)

    z_ref[0] = lax.dot_general(h_scr[...], w_ref[...], (((1,), (1,)), ((), ())),
                               preferred_element_type=F32)


def _mod_spec(m, tt):
    d = m.shape[-1]
    if m.shape[1] == 1:
        return pl.BlockSpec((1, 1, d), lambda b, i, *_: (b, 0, 0))
    return pl.BlockSpec((1, tt, d), lambda b, i, *_: (b, i, 0))


def _inproj(x, sc, sh, g, w_main, w_lr, tt):
    bsz, t, d = x.shape
    n = w_main.shape[0]
    tc = 1024
    return pl.pallas_call(
        _inproj_kernel,
        grid=(bsz, t // tt, n // tc),
        in_specs=[pl.BlockSpec((1, tt, d), lambda b, i, j: (b, i, 0)),
                  _mod_spec(sc, tt), _mod_spec(sh, tt),
                  pl.BlockSpec((1, d), lambda b, i, j: (0, 0)),
                  pl.BlockSpec((tc, d), lambda b, i, j: (j, 0)),
                  pl.BlockSpec((LANES, d), lambda b, i, j: (0, 0))],
        out_specs=[pl.BlockSpec((1, tt, tc), lambda b, i, j: (b, i, j)),
                   pl.BlockSpec((1, tt, LANES), lambda b, i, j: (b, i, 0))],
        out_shape=[jax.ShapeDtypeStruct((bsz, t, n), F32),
                   jax.ShapeDtypeStruct((bsz, t, LANES), F32)],
        scratch_shapes=[pltpu.VMEM((tt, d), BF16)],
        compiler_params=_params("parallel", "parallel", "arbitrary"),
        name="inproj",
    )(x, sc, sh, g, w_main, w_lr)


def _gla_kernel(q_ref, k_ref, v_ref, r_ref, lr_ref, wau_ref, ba_ref, gg_ref, s0_ref,
                o_ref, s_ref, st_scr, *, dk, dv):
    c = pl.program_id(1)
    rows = q_ref.shape[1]

    @pl.when(c == 0)
    def _():
        for h in range(GLA_HEADS):
            st_scr[h] = s0_ref[0, h].T

    pre = jnp.dot(lr_ref[0], wau_ref[...], preferred_element_type=F32,
                  precision=lax.Precision.HIGHEST) + ba_ref[...]
    log_a = (jnp.minimum(pre, 0.0) - jnp.log1p(jnp.exp(-jnp.abs(pre)))) * (1.0 / GLA_GATE_NORM)
    row = lax.broadcasted_iota(jnp.int32, log_a.shape, 0)
    b = log_a
    shift = 1
    while shift < rows:
        b = b + jnp.where(row >= shift, pltpu.roll(b, shift, axis=0), 0.0)
        shift *= 2
    b_last = b[rows - 1:rows, :]
    e_pos = jnp.exp(b)
    q_in = q_ref[0] * (dk ** -0.5) * e_pos
    kk = k_ref[0]
    k_in = kk * jnp.exp(-b)
    k_out = kk * jnp.exp(b_last - b)
    e_last = jnp.exp(b_last)
    causal = (lax.broadcasted_iota(jnp.int32, (rows, rows), 0)
              >= lax.broadcasted_iota(jnp.int32, (rows, rows), 1))
    gg = gg_ref[...]
    for h in range(GLA_HEADS):
        ks = slice(h * dk, (h + 1) * dk)
        vs = slice(h * dv, (h + 1) * dv)
        qh = q_in[:, ks].astype(BF16)
        vh = v_ref[0, :, vs].astype(BF16)
        st = st_scr[h]
        att = lax.dot_general(qh, k_in[:, ks].astype(BF16), (((1,), (1,)), ((), ())),
                              preferred_element_type=F32)
        att = jnp.where(causal, att, 0.0).astype(BF16)
        o = jnp.dot(att, vh, preferred_element_type=F32)
        o = o + lax.dot_general(qh, st.astype(BF16), (((1,), (1,)), ((), ())),
                                preferred_element_type=F32)
        st_scr[h] = st * e_last[:, ks] + lax.dot_general(
            vh, k_out[:, ks].astype(BF16), (((0,), (0,)), ((), ())), preferred_element_type=F32)
        o = _rms(o) * gg * _silu(r_ref[0, :, vs])
        o_ref[0, :, vs] = o.astype(o_ref.dtype)

    @pl.when(c == pl.num_programs(1) - 1)
    def _():
        for h in range(GLA_HEADS):
            s_ref[0, h] = st_scr[h].T


def _gla(z, zlr, w_au, b_alpha, g_gla_out, s0):
    bsz, t, _ = z.shape
    _, nh, dk, dv = s0.shape
    rows = min(CHUNK, t)
    kern = functools.partial(_gla_kernel, dk=dk, dv=dv)
    return pl.pallas_call(
        kern,
        grid=(bsz, t // rows),
        in_specs=[pl.BlockSpec((1, rows, nh * dk), lambda b, c: (b, c, 0)),
                  pl.BlockSpec((1, rows, nh * dk), lambda b, c: (b, c, 1)),
                  pl.BlockSpec((1, rows, nh * dv), lambda b, c: (b, c, 1)),
                  pl.BlockSpec((1, rows, nh * dv), lambda b, c: (b, c, 2)),
                  pl.BlockSpec((1, rows, LANES), lambda b, c: (b, c, 0)),
                  pl.BlockSpec((LANES, nh * dk), lambda b, c: (0, 0)),
                  pl.BlockSpec((1, nh * dk), lambda b, c: (0, 0)),
                  pl.BlockSpec((1, dv), lambda b, c: (0, 0)),
                  pl.BlockSpec((1, nh, dk, dv), lambda b, c: (b, 0, 0, 0))],
        out_specs=[pl.BlockSpec((1, rows, nh * dv), lambda b, c: (b, c, 0)),
                   pl.BlockSpec((1, nh, dk, dv), lambda b, c: (b, 0, 0, 0))],
        out_shape=[jax.ShapeDtypeStruct((bsz, t, nh * dv), BF16),
                   jax.ShapeDtypeStruct((bsz, nh, dk, dv), F32)],
        scratch_shapes=[pltpu.VMEM((nh, dv, dk), F32)],
        compiler_params=_params("parallel", "arbitrary"),
        name="gla",
    )(z, z, z, z, zlr, w_au, b_alpha, g_gla_out, s0)


def _conv_kernel(a_ref, g_ref, cache_ref, wdw_ref, bdw_ref, gln_ref, bln_ref,
                 y_ref, newc_ref, shift_scr):
    i = pl.program_id(1)
    tt = a_ref.shape[1]
    hist = CONV_WIDTH - 1
    pad = CONV_HALO - hist
    win_scr = shift_scr.at[0]

    @pl.when(i == 0)
    def _():
        win_scr[0:pad, :] = jnp.zeros((pad, win_scr.shape[1]), F32)
        win_scr[pad:CONV_HALO, :] = cache_ref[0]

    win_scr[CONV_HALO:CONV_HALO + tt, :] = a_ref[0] * _sigmoid(g_ref[0])
    full = win_scr[...]
    for s in range(1, SUBLANES):
        shift_scr[s] = pltpu.roll(full, full.shape[0] - s, axis=0)
    bdw = bdw_ref[...]
    gln = gln_ref[...]
    bln = bln_ref[...]

    rows = 2 * SUBLANES

    def chunk(ci, carry):
        r0 = pl.multiple_of(ci * rows, rows)
        acc = jnp.zeros((2, SUBLANES, win_scr.shape[1]), F32)
        for w in range(CONV_WIDTH):
            q, s = divmod(pad + w, SUBLANES)
            x = shift_scr[s, pl.ds(r0 + q * SUBLANES, rows), :]
            acc = acc + x.reshape(2, SUBLANES, x.shape[1]) * wdw_ref[w]
        acc = acc.reshape(rows, win_scr.shape[1]) + bdw
        mu = jnp.mean(acc, axis=-1, keepdims=True)
        xc = acc - mu
        yn = xc * lax.rsqrt(jnp.mean(xc * xc, axis=-1, keepdims=True) + EPS) * gln + bln
        y_ref[0, pl.ds(r0, rows), :] = _silu(yn).astype(y_ref.dtype)
        return carry

    lax.fori_loop(0, tt // rows, chunk, 0)

    @pl.when(i == pl.num_programs(1) - 1)
    def _():
        newc_ref[0] = win_scr[tt + pad:tt + CONV_HALO, :]

    tail = win_scr[tt:tt + CONV_HALO, :]
    win_scr[0:CONV_HALO, :] = tail


def _conv(z, cache, w_dw, b_dw, g_ln, b_ln, tt):
    bsz, t, _ = z.shape
    _, hist, ch = cache.shape
    return pl.pallas_call(
        _conv_kernel,
        grid=(bsz, t // tt),
        in_specs=[pl.BlockSpec((1, tt, ch), lambda b, i: (b, i, 6)),
                  pl.BlockSpec((1, tt, ch), lambda b, i: (b, i, 7)),
                  pl.BlockSpec((1, hist, ch), lambda b, i: (b, 0, 0)),
                  pl.BlockSpec((CONV_WIDTH, SUBLANES, ch), lambda b, i: (0, 0, 0)),
                  pl.BlockSpec((1, ch), lambda b, i: (0, 0)),
                  pl.BlockSpec((1, ch), lambda b, i: (0, 0)),
                  pl.BlockSpec((1, ch), lambda b, i: (0, 0))],
        out_specs=[pl.BlockSpec((1, tt, ch), lambda b, i: (b, i, 0)),
                   pl.BlockSpec((1, hist, ch), lambda b, i: (b, 0, 0))],
        out_shape=[jax.ShapeDtypeStruct((bsz, t, ch), BF16),
                   jax.ShapeDtypeStruct((bsz, hist, ch), F32)],
        scratch_shapes=[pltpu.VMEM((SUBLANES, tt + CONV_HALO, ch), F32)],
        compiler_params=_params("parallel", "arbitrary"),
        name="conv",
    )(z, z, cache, w_dw, b_dw, g_ln, b_ln)


def _merge_kernel(oa_ref, cb_ref, ga_ref, gb_ref, x_ref, gt1_ref, sc2_ref, sh2_ref,
                  gpost_ref, gpre_ref, wgo_ref, wco_ref, wout_ref, x1_ref, h2_ref):
    ya = jnp.dot(oa_ref[0], wgo_ref[...], preferred_element_type=F32)
    yb = jnp.dot(cb_ref[0], wco_ref[...], preferred_element_type=F32)
    m = _sigmoid(ga_ref[0]) * ya + _sigmoid(gb_ref[0]) * yb
    mix = jnp.dot(m.astype(BF16), wout_ref[...], preferred_element_type=F32)
    x1 = x_ref[0] + gt1_ref[0] * (_rms(mix) * gpost_ref[...])
    x1_ref[0] = x1
    h2 = _rms(x1) * gpre_ref[...]
    h2_ref[0] = (h2 * (1.0 + sc2_ref[0]) + sh2_ref[0]).astype(h2_ref.dtype)


def _merge(oa, cb, z, x, gt1, sc2, sh2, g_post, g_pre, w_go, w_co, w_out, tt):
    bsz, t, d = x.shape
    ch = cb.shape[-1]
    tok = lambda w: pl.BlockSpec((1, tt, w), lambda b, i: (b, i, 0))
    return pl.pallas_call(
        _merge_kernel,
        grid=(bsz, t // tt),
        in_specs=[tok(d), tok(ch),
                  pl.BlockSpec((1, tt, d), lambda b, i: (b, i, 4)),
                  pl.BlockSpec((1, tt, d), lambda b, i: (b, i, 5)),
                  tok(d), _mod_spec(gt1, tt), _mod_spec(sc2, tt), _mod_spec(sh2, tt),
                  _const_spec((1, d)), _const_spec((1, d)),
                  _const_spec(w_go.shape), _const_spec(w_co.shape), _const_spec(w_out.shape)],
        out_specs=[tok(d), tok(d)],
        out_shape=[jax.ShapeDtypeStruct((bsz, t, d), F32),
                   jax.ShapeDtypeStruct((bsz, t, d), BF16)],
        compiler_params=_params("parallel", "parallel"),
        name="merge",
    )(oa, cb, z, z, x, gt1, sc2, sh2, g_post, g_pre, w_go, w_co, w_out)


def _keyfold_kernel(ks_ref, wq_ref, o_ref):
    o_ref[...] = lax.dot_general(ks_ref[0], wq_ref[...], (((1,), (1,)), ((), ())),
                                 preferred_element_type=F32,
                                 precision=lax.Precision.HIGHEST).astype(o_ref.dtype)


def _keyfold(k_sub, w_pq):
    d = w_pq.shape[0]
    _, nk, dq = k_sub.shape
    nblk = w_pq.shape[1] // dq
    return pl.pallas_call(
        _keyfold_kernel,
        grid=(nblk,),
        in_specs=[pl.BlockSpec((1, nk, dq), lambda j: (j % 2, 0, 0)),
                  pl.BlockSpec((d, dq), lambda j: (0, j))],
        out_specs=pl.BlockSpec((nk, d), lambda j: (j, 0)),
        out_shape=jax.ShapeDtypeStruct((nblk * nk, d), BF16),
        compiler_params=_params("arbitrary"),
        name="keyfold",
    )(k_sub, w_pq)


def _top16(s):
    rows = s.shape[0]
    rid = lax.broadcasted_iota(jnp.int32, s.shape, 0).astype(F32)
    vals, idxs = [], []
    for _ in range(PEER_TOPK):
        m = jnp.max(s, axis=0, keepdims=True)
        am = jnp.min(jnp.where(s == m, rid, float(rows)), axis=0, keepdims=True)
        vals.append(m)
        idxs.append(am)
        s = jnp.where(rid == am, -jnp.inf, s)
    return vals, idxs


def _stack_rows(rows_list, n):
    k = len(rows_list)
    rid = lax.broadcasted_iota(jnp.int32, (k, n), 0)
    out = jnp.zeros((k, n), F32)
    for r, v in enumerate(rows_list):
        out = jnp.where(rid == r, v, out)
    return out


def _topk_columns(s1, s2):
    tn = s1.shape[1]
    nk = PEER_NKEYS
    v1, i1 = _top16(s1)
    v2, i2 = _top16(s2)
    v2a = _stack_rows(v2, tn)
    i2a = _stack_rows(i2, tn)
    v1a = _stack_rows(v1, tn)
    i1a = _stack_rows(i1, tn)
    sub = lax.broadcasted_iota(jnp.int32, (SUBLANES, tn), 0)
    cand, cidx = [], []
    for a in range(SUBLANES):
        lim = PEER_TOPK // (a + 1)
        for b0 in range(0, lim, SUBLANES):
            cv = v1[a] + v2a[b0:b0 + SUBLANES]
            ci = i1[a] * float(nk) + i2a[b0:b0 + SUBLANES]
            if lim - b0 < SUBLANES:
                cv = jnp.where(sub < lim - b0, cv, -jnp.inf)
            cand.append(cv)
            cidx.append(ci)
    cand.append(v1a[SUBLANES:] + v2[0])
    cidx.append(i1a[SUBLANES:] * float(nk) + i2[0])
    cand = jnp.concatenate(cand, axis=0)
    cidx = jnp.concatenate(cidx, axis=0)
    rid = lax.broadcasted_iota(jnp.int32, cand.shape, 0).astype(F32)
    sc, ids = [], []
    for _ in range(PEER_TOPK):
        m = jnp.max(cand, axis=0, keepdims=True)
        am = jnp.min(jnp.where(cand == m, rid, float(cand.shape[0])), axis=0, keepdims=True)
        hit = rid == am
        sc.append(m)
        ids.append(jnp.sum(jnp.where(hit, cidx, 0.0), axis=0, keepdims=True))
        cand = jnp.where(hit, -jnp.inf, cand)
    sca = _stack_rows(sc, tn)
    ida = _stack_rows(ids, tn)
    e = jnp.exp(sca - sc[0])
    wts = e / jnp.sum(e, axis=0, keepdims=True)
    ia = jnp.floor(ida * (1.0 / nk))
    return wts, ia, ida - ia * float(nk)


def _topk_kernel(h_ref, tab_ref, w_ref, a_ref, b_ref, s_scr, w_scr, a_scr, b_scr):
    hd = pl.program_id(1)
    tn = h_ref.shape[0]
    nk = PEER_NKEYS
    s_scr[...] = lax.dot_general(tab_ref[...], h_ref[...], (((1,), (1,)), ((), ())),
                                 preferred_element_type=F32)
    r0 = pl.multiple_of(hd * PEER_TOPK, PEER_TOPK)

    width = min(tn, 2 * LANES)

    def lane_block(c, carry):
        cols = pl.ds(pl.multiple_of(c * width, width), width)
        wts, ia, ib = _topk_columns(s_scr[0:nk, cols], s_scr[nk:2 * nk, cols])
        w_scr[pl.ds(r0, PEER_TOPK), cols] = wts
        a_scr[pl.ds(r0, PEER_TOPK), cols] = ia
        b_scr[pl.ds(r0, PEER_TOPK), cols] = ib
        return carry

    lax.fori_loop(0, tn // width, lane_block, 0)

    @pl.when(hd == pl.num_programs(1) - 1)
    def _():
        w_ref[...] = w_scr[...].T
        a_ref[...] = a_scr[...].T
        b_ref[...] = b_scr[...].T


def _topk(h2, table, tn):
    n, d = h2.shape
    nsel = PEER_HEADS * PEER_TOPK
    out = jax.ShapeDtypeStruct((n, nsel), F32)
    ospec = pl.BlockSpec((tn, nsel), lambda i, h: (i, 0))
    return pl.pallas_call(
        _topk_kernel,
        grid=(n // tn, PEER_HEADS),
        in_specs=[pl.BlockSpec((tn, d), lambda i, h: (i, 0)),
                  pl.BlockSpec((2 * PEER_NKEYS, d), lambda i, h: (h, 0))],
        out_specs=[ospec, ospec, ospec],
        out_shape=[out, out, out],
        scratch_shapes=[pltpu.VMEM((2 * PEER_NKEYS, tn), F32)] + [pltpu.VMEM((nsel, tn), F32)] * 3,
        compiler_params=_params("parallel", "arbitrary"),
        name="topk",
    )(h2, table)


def _act_kernel(h_ref, u_ref, w_ref, a_ref, b_ref, coef_ref, sel_scr):
    k = pl.program_id(1)
    te = u_ref.shape[0]

    @pl.when(k == 0)
    def _():
        sel_scr[...] = jnp.zeros(sel_scr.shape, F32)

    ia = a_ref[...]
    ib = b_ref[...].astype(jnp.int32)
    sel = sel_scr[...]
    base = (k * (te // LANES)).astype(F32)
    for e0 in range(0, te, MXU_WIDTH):
        s = lax.dot_general(h_ref[...], u_ref[e0:e0 + MXU_WIDTH, :].astype(BF16),
                            (((1,), (1,)), ((), ())), preferred_element_type=F32)
        for j in range(MXU_WIDTH // LANES):
            g = jnp.take_along_axis(s[:, j * LANES:(j + 1) * LANES], ib, axis=1)
            sel = jnp.where(ia == base + float(e0 // LANES + j), g, sel)
    sel_scr[...] = sel

    @pl.when(k == pl.num_programs(1) - 1)
    def _():
        act = 0.5 * sel * (1.0 + lax.erf(sel * (2.0 ** -0.5)))
        coef_ref[...] = w_ref[...] * act


def _expert_act(h2, u_bf, wts, ia, ib, tn, te):
    n, d = h2.shape
    ne = u_bf.shape[0]
    nsel = wts.shape[1]
    sspec = pl.BlockSpec((tn, nsel), lambda i, k: (i, 0))
    return pl.pallas_call(
        _act_kernel,
        grid=(n // tn, ne // te),
        in_specs=[pl.BlockSpec((tn, d), lambda i, k: (i, 0)),
                  pl.BlockSpec((te, d), lambda i, k: (k, 0)),
                  sspec, sspec, sspec],
        out_specs=sspec,
        out_shape=jax.ShapeDtypeStruct((n, nsel), F32),
        scratch_shapes=[pltpu.VMEM((tn, nsel), F32)],
        compiler_params=_params("parallel", "arbitrary"),
        name="expert_act",
    )(h2, u_bf, wts, ia, ib)


def _mixmat_kernel(coef_ref, a_ref, b_ref, p_ref, p_scr):
    tn = coef_ref.shape[0]
    nk = PEER_NKEYS
    kid = lax.broadcasted_iota(jnp.int32, (nk, coef_ref.shape[1]), 0).astype(F32).astype(BF16)
    one = jnp.ones((nk, coef_ref.shape[1]), BF16)
    zero = jnp.zeros((nk, coef_ref.shape[1]), BF16)

    def group(r, carry):
        t0 = pl.multiple_of(r * SUBLANES, SUBLANES)
        for s in range(SUBLANES):
            c = coef_ref[pl.ds(t0 + s, 1), :].astype(BF16)
            ia = a_ref[pl.ds(t0 + s, 1), :].astype(BF16)
            ib = b_ref[pl.ds(t0 + s, 1), :].astype(BF16)
            lhs = jnp.where(ia == kid, c, zero)
            rhs = jnp.where(ib == kid, one, zero)
            p_scr[pl.ds(r * (nk * SUBLANES) + s, nk, stride=SUBLANES), :] = lax.dot_general(
                lhs, rhs, (((1,), (1,)), ((), ())), preferred_element_type=F32)
        return carry

    lax.fori_loop(0, tn // SUBLANES, group, 0)

    def emit(rr, carry):
        lo = pl.multiple_of(rr * (2 * nk * SUBLANES), SUBLANES)
        hi = lo + nk * SUBLANES
        row = pl.multiple_of(rr * 2 * SUBLANES, 2 * SUBLANES)
        for j in range(nk):
            tile = jnp.concatenate([p_scr[pl.ds(lo + j * SUBLANES, SUBLANES), :],
                                    p_scr[pl.ds(hi + j * SUBLANES, SUBLANES), :]], axis=0)
            p_ref[pl.ds(row, 2 * SUBLANES), j * nk:(j + 1) * nk] = tile.astype(p_ref.dtype)
        return carry

    lax.fori_loop(0, tn // (2 * SUBLANES), emit, 0)


def _mixmat(coef, ia, ib, tn):
    n, nsel = coef.shape
    ne = PEER_NKEYS * PEER_NKEYS
    sspec = pl.BlockSpec((tn, nsel), lambda i: (i, 0))
    return pl.pallas_call(
        _mixmat_kernel,
        grid=(n // tn,),
        in_specs=[sspec, sspec, sspec],
        out_specs=pl.BlockSpec((tn, ne), lambda i: (i, 0)),
        out_shape=jax.ShapeDtypeStruct((n, ne), BF16),
        scratch_shapes=[pltpu.VMEM((tn * PEER_NKEYS, PEER_NKEYS), F32)],
        compiler_params=_params("parallel"),
        name="mixmat",
    )(coef, ia, ib)


def _value_kernel(p_ref, v_ref, x1_ref, gt2_ref, g_ref, y_ref, acc_scr):
    k = pl.program_id(2)

    @pl.when(k == 0)
    def _():
        acc_scr[...] = jnp.zeros(acc_scr.shape, F32)

    acc_scr[...] += jnp.dot(p_ref[0], v_ref[...], preferred_element_type=F32)

    @pl.when(k == pl.num_programs(2) - 1)
    def _():
        y_ref[0] = x1_ref[0] + gt2_ref[0] * (_rms(acc_scr[...]) * g_ref[...])


def _value(p, v_bf, x1, gt2, g_post, tt, tk):
    bsz, t, d = x1.shape
    ne = v_bf.shape[0]
    return pl.pallas_call(
        _value_kernel,
        grid=(bsz, t // tt, ne // tk),
        in_specs=[pl.BlockSpec((1, tt, tk), lambda b, i, k: (b, i, k)),
                  pl.BlockSpec((tk, d), lambda b, i, k: (k, 0)),
                  pl.BlockSpec((1, tt, d), lambda b, i, k: (b, i, 0),
                               pipeline_mode=pl.Buffered(1)),
                  _mod_spec(gt2, tt),
                  pl.BlockSpec((1, d), lambda b, i, k: (0, 0))],
        out_specs=pl.BlockSpec((1, tt, d), lambda b, i, k: (b, i, 0),
                               pipeline_mode=pl.Buffered(1)),
        out_shape=jax.ShapeDtypeStruct((bsz, t, d), F32),
        scratch_shapes=[pltpu.VMEM((tt, d), F32)],
        compiler_params=_params("parallel", "parallel", "arbitrary"),
        name="value",
    )(p, v_bf, x1, gt2, g_post)


def _layer(x, mods, s0, cache, wts, *, rec_shape, tt):
    sh1, sc1, gt1, sh2, sc2, gt2 = mods
    bsz, t, d = x.shape
    z, zlr = _inproj(x, sc1, sh1, wts["g_pre_mix"], wts["w_main"], wts["w_lr"], tt)
    rs, rt = rec_shape
    zr = z.reshape(rs, rt, z.shape[-1])
    oa, s_new = _gla(zr, zlr.reshape(rs, rt, LANES), wts["w_au"], wts["b_alpha"],
                     wts["g_gla_out"], s0)
    cb, c_new = _conv(zr, cache, wts["w_dw"], wts["b_dw"], wts["g_conv_ln"], wts["b_conv_ln"],
                      min(rt, 256))
    x1, h2 = _merge(oa.reshape(bsz, t, -1), cb.reshape(bsz, t, -1), z, x, gt1, sc2, sh2,
                    wts["g_post_mix"], wts["g_pre_ffn"], wts["w_gla_o"], wts["w_conv_o"],
                    wts["w_out"], min(t, 256))
    n = bsz * t
    h2f = h2.reshape(n, d)
    pw, ia, ib = _topk(h2f, wts["key_table"], min(n, 512))
    coef = _expert_act(h2f, wts["u_exp"], pw, ia, ib, min(n, 1024), 512)
    p = _mixmat(coef, ia, ib, 128)
    y = _value(p.reshape(bsz, t, -1), wts["v_bf"], x1, gt2, wts["g_post_ffn"], min(t, 512), 2048)
    return y, s_new, c_new


def kernel(x_prompt, x_sample, state_gla, cache_conv, c_prompt, c_sample, w_ada, b_ada, g_pre_mix,
           g_post_mix, w_in, w_alpha_up, b_alpha, g_gla_out, w_gla_o, w_dw, b_dw, g_conv_ln,
           b_conv_ln, w_conv_o, w_out, g_pre_ffn, g_post_ffn, w_pq, k_sub1, k_sub2, u_exp, v_exp):
    depth = w_ada.shape[0]
    bp, tp, d = x_prompt.shape
    bs, ts, _ = x_sample.shape
    nh, dk, dv = state_gla.shape[2:]
    xp, xs = x_prompt, x_sample.reshape(1, bs * ts, d)
    outs = [[], [], [], []]
    for l in range(depth):
        lr0 = 2 * nh * dk + 2 * nh * dv
        wl = w_in[l].T
        row = lambda a: a[l].reshape(1, -1)
        wts = dict(
            w_main=jnp.concatenate([wl[:lr0], wl[lr0 + GLA_GATE_RANK:]], axis=0).astype(BF16),
            w_lr=jnp.pad(wl[lr0:lr0 + GLA_GATE_RANK],
                         ((0, LANES - GLA_GATE_RANK), (0, 0))).astype(BF16),
            w_au=jnp.pad(w_alpha_up[l], ((0, LANES - GLA_GATE_RANK), (0, 0))),
            b_alpha=row(b_alpha), g_gla_out=row(g_gla_out), g_pre_mix=row(g_pre_mix),
            g_post_mix=row(g_post_mix), g_pre_ffn=row(g_pre_ffn), g_post_ffn=row(g_post_ffn),
            w_dw=jnp.broadcast_to(w_dw[l][:, None, :], (CONV_WIDTH, SUBLANES, w_dw.shape[-1])),
            b_dw=row(b_dw), g_conv_ln=row(g_conv_ln), b_conv_ln=row(b_conv_ln),
            w_gla_o=w_gla_o[l].astype(BF16), w_conv_o=w_conv_o[l].astype(BF16),
            w_out=w_out[l].astype(BF16),
            key_table=_keyfold(jnp.stack([k_sub1[l], k_sub2[l]]), w_pq[l]),
            u_exp=u_exp[l], v_bf=v_exp[l].astype(BF16),
        )
        c_all = jnp.concatenate(
            [c_prompt, c_sample, jnp.zeros((-(bp + bs) % SUBLANES, d), F32)], axis=0)
        mod = _ada(c_all, w_ada[l], b_ada[l])
        mods_p = [m.reshape(bp, 1, d) for m in jnp.split(mod[:bp], 6, axis=-1)]
        mods_s = [jnp.repeat(m, ts, axis=0).reshape(1, bs * ts, d)
                  for m in jnp.split(mod[bp:bp + bs], 6, axis=-1)]
        s0_p = jnp.zeros((bp, nh, dk, dv), F32)
        cache0_p = jnp.zeros((bp, CONV_WIDTH - 1, cache_conv.shape[-1]), F32)
        xp, s_p, cb_p = _layer(xp, mods_p, s0_p, cache0_p, wts, rec_shape=(bp, tp), tt=1024)
        xs, s_s, cb_s = _layer(xs, mods_s, state_gla[l], cache_conv[l], wts,
                               rec_shape=(bs, ts), tt=bs * ts)
        for o, v in zip(outs, (s_p, cb_p, s_s, cb_s)):
            o.append(v)
    return (xp, xs.reshape(bs, ts, d), jnp.stack(outs[0]), jnp.stack(outs[1]),
            jnp.stack(outs[2]), jnp.stack(outs[3]))
```

```python
import functools

import jax
import jax.numpy as jnp
from jax import lax
from jax.experimental import pallas as pl
from jax.experimental.pallas import tpu as pltpu

F32 = jnp.float32
BF16 = jnp.bfloat16

CHUNK = 64
GLA_HEADS = 4
GLA_GATE_RANK = 16
GLA_GATE_NORM = 16.0
CONV_WIDTH = 31
PEER_HEADS = 8
PEER_NKEYS = 128
PEER_TOPK = 16
EPS = 1e-6

LANES = 128
SUBLANES = 8
MXU_WIDTH = 256
CONV_HALO = 32
VMEM_LIMIT = 52 * 1024 * 1024


def _params(*sem):
    return pltpu.CompilerParams(dimension_semantics=sem, vmem_limit_bytes=VMEM_LIMIT)


def _const_spec(shape):
    nd = len(shape)
    return pl.BlockSpec(shape, lambda *_: (0,) * nd, pipeline_mode=pl.Buffered(1))


def _rms(x):
    return x * lax.rsqrt(jnp.mean(x * x, axis=-1, keepdims=True) + EPS)


def _sigmoid(x):
    return 1.0 / (1.0 + jnp.exp(-x))


def _silu(x):
    return x * _sigmoid(x)


def _ada_kernel(c_ref, w_ref, b_ref, o_ref):
    c = c_ref[...]
    o_ref[...] = jnp.dot(_silu(c), w_ref[...], preferred_element_type=F32,
                         precision=lax.Precision.HIGHEST) + b_ref[...]


def _ada(c, w_ada, b_ada):
    g, d = c.shape
    n = w_ada.shape[1]
    tc = 1024
    return pl.pallas_call(
        _ada_kernel,
        grid=(n // tc,),
        in_specs=[pl.BlockSpec((g, d), lambda j: (0, 0)),
                  pl.BlockSpec((d, tc), lambda j: (0, j)),
                  pl.BlockSpec((1, tc), lambda j: (0, j))],
        out_specs=pl.BlockSpec((g, tc), lambda j: (0, j)),
        out_shape=jax.ShapeDtypeStruct((g, n), F32),
        compiler_params=_params("arbitrary"),
        name="ada",
    )(c, w_ada, b_ada.reshape(1, n))


def _inproj_kernel(x_ref, sc_ref, sh_ref, g_ref, w_ref, wlr_ref, z_ref, zlr_ref, h_scr):
    @pl.when(pl.program_id(2) == 0)
    def _():
        h = _rms(x_ref[0]) * g_ref[...]
        h = (h * (1.0 + sc_ref[0]) + sh_ref[0]).astype(BF16)
        h_scr[...] = h
        zlr_ref[0] = lax.dot_general(h, wlr_ref[...], (((1,), (1,)), ((), ())),
                                     preferred_element_type=F32)

    z_ref[0] = lax.dot_general(h_scr[...], w_ref[...].astype(BF16), (((1,), (1,)), ((), ())),
                               preferred_element_type=F32)


def _mod_spec(m, tt):
    d = m.shape[-1]
    if m.shape[1] == 1:
        return pl.BlockSpec((1, 1, d), lambda b, i, *_: (b, 0, 0))
    return pl.BlockSpec((1, tt, d), lambda b, i, *_: (b, i, 0))


def _inproj(x, sc, sh, g, w_rows, w_lr, lr0, tt):
    bsz, t, d = x.shape
    n = w_rows.shape[0] - GLA_GATE_RANK
    tc = 1024
    assert lr0 % tc == 0 and n == 2 * lr0
    wrow = lambda b, i, j: (
        pl.multiple_of(j * tc + (j // (lr0 // tc)) * GLA_GATE_RANK, GLA_GATE_RANK), 0)
    return pl.pallas_call(
        _inproj_kernel,
        grid=(bsz, t // tt, n // tc),
        in_specs=[pl.BlockSpec((1, tt, d), lambda b, i, j: (b, i, 0)),
                  _mod_spec(sc, tt), _mod_spec(sh, tt),
                  pl.BlockSpec((1, d), lambda b, i, j: (0, 0)),
                  pl.BlockSpec((pl.Element(tc), pl.Element(d)), wrow),
                  pl.BlockSpec((LANES, d), lambda b, i, j: (0, 0))],
        out_specs=[pl.BlockSpec((1, tt, tc), lambda b, i, j: (b, i, j)),
                   pl.BlockSpec((1, tt, LANES), lambda b, i, j: (b, i, 0))],
        out_shape=[jax.ShapeDtypeStruct((bsz, t, n), F32),
                   jax.ShapeDtypeStruct((bsz, t, LANES), F32)],
        scratch_shapes=[pltpu.VMEM((tt, d), BF16)],
        compiler_params=_params("parallel", "parallel", "arbitrary"),
        name="inproj",
    )(x, sc, sh, g, w_rows, w_lr)


def _gla_kernel(q_ref, k_ref, v_ref, r_ref, lr_ref, wau_ref, ba_ref, gg_ref, s0_ref,
                o_ref, s_ref, st_scr, *, dk, dv):
    c = pl.program_id(1)
    rows = q_ref.shape[1]

    @pl.when(c == 0)
    def _():
        for h in range(GLA_HEADS):
            st_scr[h] = s0_ref[0, h].T

    pre = jnp.dot(lr_ref[0], wau_ref[...], preferred_element_type=F32,
                  precision=lax.Precision.HIGHEST) + ba_ref[...]
    log_a = (jnp.minimum(pre, 0.0) - jnp.log1p(jnp.exp(-jnp.abs(pre)))) * (1.0 / GLA_GATE_NORM)
    row = lax.broadcasted_iota(jnp.int32, log_a.shape, 0)
    b = log_a
    shift = 1
    while shift < rows:
        b = b + jnp.where(row >= shift, pltpu.roll(b, shift, axis=0), 0.0)
        shift *= 2
    b_last = b[rows - 1:rows, :]
    e_pos = jnp.exp(b)
    q_in = q_ref[0] * (dk ** -0.5) * e_pos
    kk = k_ref[0]
    k_in = kk * jnp.exp(-b)
    k_out = kk * jnp.exp(b_last - b)
    e_last = jnp.exp(b_last)
    causal = (lax.broadcasted_iota(jnp.int32, (rows, rows), 0)
              >= lax.broadcasted_iota(jnp.int32, (rows, rows), 1))
    gg = gg_ref[...]
    for h in range(GLA_HEADS):
        ks = slice(h * dk, (h + 1) * dk)
        vs = slice(h * dv, (h + 1) * dv)
        qh = q_in[:, ks].astype(BF16)
        vh = v_ref[0, :, vs].astype(BF16)
        st = st_scr[h]
        att = lax.dot_general(qh, k_in[:, ks].astype(BF16), (((1,), (1,)), ((), ())),
                              preferred_element_type=F32)
        att = jnp.where(causal, att, 0.0).astype(BF16)
        o = jnp.dot(att, vh, preferred_element_type=F32)
        o = o + lax.dot_general(qh, st.astype(BF16), (((1,), (1,)), ((), ())),
                                preferred_element_type=F32)
        st_scr[h] = st * e_last[:, ks] + lax.dot_general(
            vh, k_out[:, ks].astype(BF16), (((0,), (0,)), ((), ())), preferred_element_type=F32)
        o = _rms(o) * gg * _silu(r_ref[0, :, vs])
        o_ref[0, :, vs] = o.astype(o_ref.dtype)

    @pl.when(c == pl.num_programs(1) - 1)
    def _():
        for h in range(GLA_HEADS):
            s_ref[0, h] = st_scr[h].T


def _gla(z, zlr, w_au, b_alpha, g_gla_out, s0):
    bsz, t, _ = z.shape
    _, nh, dk, dv = s0.shape
    rows = min(CHUNK, t)
    kern = functools.partial(_gla_kernel, dk=dk, dv=dv)
    return pl.pallas_call(
        kern,
        grid=(bsz, t // rows),
        in_specs=[pl.BlockSpec((1, rows, nh * dk), lambda b, c: (b, c, 0)),
                  pl.BlockSpec((1, rows, nh * dk), lambda b, c: (b, c, 1)),
                  pl.BlockSpec((1, rows, nh * dv), lambda b, c: (b, c, 1)),
                  pl.BlockSpec((1, rows, nh * dv), lambda b, c: (b, c, 2)),
                  pl.BlockSpec((1, rows, LANES), lambda b, c: (b, c, 0)),
                  pl.BlockSpec((LANES, nh * dk), lambda b, c: (0, 0)),
                  pl.BlockSpec((1, nh * dk), lambda b, c: (0, 0)),
                  pl.BlockSpec((1, dv), lambda b, c: (0, 0)),
                  pl.BlockSpec((1, nh, dk, dv), lambda b, c: (b, 0, 0, 0))],
        out_specs=[pl.BlockSpec((1, rows, nh * dv), lambda b, c: (b, c, 0)),
                   pl.BlockSpec((1, nh, dk, dv), lambda b, c: (b, 0, 0, 0))],
        out_shape=[jax.ShapeDtypeStruct((bsz, t, nh * dv), BF16),
                   jax.ShapeDtypeStruct((bsz, nh, dk, dv), F32)],
        scratch_shapes=[pltpu.VMEM((nh, dv, dk), F32)],
        compiler_params=_params("parallel", "arbitrary"),
        name="gla",
    )(z, z, z, z, zlr, w_au, b_alpha, g_gla_out, s0)


def _conv_kernel(a_ref, g_ref, cache_ref, wdw_ref, bdw_ref, gln_ref, bln_ref,
                 y_ref, newc_ref, shift_scr, conv_scr):
    i = pl.program_id(1)
    tt = a_ref.shape[1]
    hist = CONV_WIDTH - 1
    pad = CONV_HALO - hist
    win_scr = shift_scr.at[0]

    @pl.when(i == 0)
    def _():
        win_scr[0:pad, :] = jnp.zeros((pad, win_scr.shape[1]), F32)
        win_scr[pad:CONV_HALO, :] = cache_ref[0]

    win_scr[CONV_HALO:CONV_HALO + tt, :] = a_ref[0] * _sigmoid(g_ref[0])
    full = win_scr[...]
    for s in range(1, SUBLANES):
        shift_scr[s] = pltpu.roll(full, full.shape[0] - s, axis=0)
    bdw = bdw_ref[...]
    gln = gln_ref[...]
    bln = bln_ref[...]

    rows = 2 * SUBLANES

    def chunk(ci, carry):
        r0 = pl.multiple_of(ci * rows, rows)
        acc = jnp.zeros((2, SUBLANES, win_scr.shape[1]), F32)
        for w in range(CONV_WIDTH):
            q, s = divmod(pad + w, SUBLANES)
            x = shift_scr[s, pl.ds(r0 + q * SUBLANES, rows), :]
            acc = acc + x.reshape(2, SUBLANES, x.shape[1]) * wdw_ref[w]
        conv_scr[pl.ds(r0, rows), :] = acc.reshape(rows, win_scr.shape[1])
        return carry

    lax.fori_loop(0, tt // rows, chunk, 0)
    acc = conv_scr[...] + bdw
    mu = jnp.mean(acc, axis=-1, keepdims=True)
    xc = acc - mu
    yn = xc * lax.rsqrt(jnp.mean(xc * xc, axis=-1, keepdims=True) + EPS) * gln + bln
    y_ref[0] = _silu(yn).astype(y_ref.dtype)

    @pl.when(i == pl.num_programs(1) - 1)
    def _():
        newc_ref[0] = win_scr[tt + pad:tt + CONV_HALO, :]

    tail = win_scr[tt:tt + CONV_HALO, :]
    win_scr[0:CONV_HALO, :] = tail


def _conv(z, cache, w_dw, b_dw, g_ln, b_ln, tt):
    bsz, t, _ = z.shape
    _, hist, ch = cache.shape
    return pl.pallas_call(
        _conv_kernel,
        grid=(bsz, t // tt),
        in_specs=[pl.BlockSpec((1, tt, ch), lambda b, i: (b, i, 6)),
                  pl.BlockSpec((1, tt, ch), lambda b, i: (b, i, 7)),
                  pl.BlockSpec((1, hist, ch), lambda b, i: (b, 0, 0)),
                  pl.BlockSpec((CONV_WIDTH, SUBLANES, ch), lambda b, i: (0, 0, 0)),
                  pl.BlockSpec((1, ch), lambda b, i: (0, 0)),
                  pl.BlockSpec((1, ch), lambda b, i: (0, 0)),
                  pl.BlockSpec((1, ch), lambda b, i: (0, 0))],
        out_specs=[pl.BlockSpec((1, tt, ch), lambda b, i: (b, i, 0)),
                   pl.BlockSpec((1, hist, ch), lambda b, i: (b, 0, 0))],
        out_shape=[jax.ShapeDtypeStruct((bsz, t, ch), BF16),
                   jax.ShapeDtypeStruct((bsz, hist, ch), F32)],
        scratch_shapes=[pltpu.VMEM((SUBLANES, tt + CONV_HALO, ch), F32), pltpu.VMEM((tt, ch), F32)],
        compiler_params=_params("parallel", "arbitrary"),
        name="conv",
    )(z, z, cache, w_dw, b_dw, g_ln, b_ln)


def _merge_kernel(oa_ref, cb_ref, ga_ref, gb_ref, x_ref, gt1_ref, sc2_ref, sh2_ref,
                  gpost_ref, gpre_ref, wgo_ref, wco_ref, wout_ref, x1_ref, h2_ref):
    ya = jnp.dot(oa_ref[0], wgo_ref[...], preferred_element_type=F32)
    yb = jnp.dot(cb_ref[0], wco_ref[...], preferred_element_type=F32)
    m = _sigmoid(ga_ref[0]) * ya + _sigmoid(gb_ref[0]) * yb
    mix = jnp.dot(m.astype(BF16), wout_ref[...], preferred_element_type=F32)
    x1 = x_ref[0] + gt1_ref[0] * (_rms(mix) * gpost_ref[...])
    x1_ref[0] = x1
    h2 = _rms(x1) * gpre_ref[...]
    h2_ref[0] = (h2 * (1.0 + sc2_ref[0]) + sh2_ref[0]).astype(h2_ref.dtype)


def _merge(oa, cb, z, x, gt1, sc2, sh2, g_post, g_pre, w_go, w_co, w_out, tt):
    bsz, t, d = x.shape
    ch = cb.shape[-1]
    tok = lambda w: pl.BlockSpec((1, tt, w), lambda b, i: (b, i, 0))
    return pl.pallas_call(
        _merge_kernel,
        grid=(bsz, t // tt),
        in_specs=[tok(d), tok(ch),
                  pl.BlockSpec((1, tt, d), lambda b, i: (b, i, 4)),
                  pl.BlockSpec((1, tt, d), lambda b, i: (b, i, 5)),
                  tok(d), _mod_spec(gt1, tt), _mod_spec(sc2, tt), _mod_spec(sh2, tt),
                  _const_spec((1, d)), _const_spec((1, d)),
                  _const_spec(w_go.shape), _const_spec(w_co.shape), _const_spec(w_out.shape)],
        out_specs=[tok(d), tok(d)],
        out_shape=[jax.ShapeDtypeStruct((bsz, t, d), F32),
                   jax.ShapeDtypeStruct((bsz, t, d), BF16)],
        compiler_params=_params("parallel", "parallel"),
        name="merge",
    )(oa, cb, z, z, x, gt1, sc2, sh2, g_post, g_pre, w_go, w_co, w_out)


def _keyfold_kernel(ks_ref, wq_ref, o_ref):
    o_ref[...] = lax.dot_general(ks_ref[0], wq_ref[...], (((1,), (1,)), ((), ())),
                                 preferred_element_type=F32,
                                 precision=lax.Precision.HIGHEST).astype(o_ref.dtype)


def _keyfold(k_sub, w_pq):
    d = w_pq.shape[0]
    _, nk, dq = k_sub.shape
    nblk = w_pq.shape[1] // dq
    return pl.pallas_call(
        _keyfold_kernel,
        grid=(nblk,),
        in_specs=[pl.BlockSpec((1, nk, dq), lambda j: (j % 2, 0, 0)),
                  pl.BlockSpec((d, dq), lambda j: (0, j))],
        out_specs=pl.BlockSpec((nk, d), lambda j: (j, 0)),
        out_shape=jax.ShapeDtypeStruct((nblk * nk, d), BF16),
        compiler_params=_params("arbitrary"),
        name="keyfold",
    )(k_sub, w_pq)


def _top16(s):
    rows = s.shape[0]
    rid = lax.broadcasted_iota(jnp.int32, s.shape, 0).astype(F32)
    vals, idxs = [], []
    for _ in range(PEER_TOPK):
        m = jnp.max(s, axis=0, keepdims=True)
        am = jnp.min(jnp.where(s == m, rid, float(rows)), axis=0, keepdims=True)
        vals.append(m)
        idxs.append(am)
        s = jnp.where(rid == am, -jnp.inf, s)
    return vals, idxs


def _stack_rows(rows_list, n):
    k = len(rows_list)
    rid = lax.broadcasted_iota(jnp.int32, (k, n), 0)
    out = jnp.zeros((k, n), F32)
    for r, v in enumerate(rows_list):
        out = jnp.where(rid == r, v, out)
    return out


def _topk_columns(s1, s2):
    tn = s1.shape[1]
    nk = PEER_NKEYS
    v1, i1 = _top16(s1)
    v2, i2 = _top16(s2)
    v2a = _stack_rows(v2, tn)
    i2a = _stack_rows(i2, tn)
    v1a = _stack_rows(v1, tn)
    i1a = _stack_rows(i1, tn)
    sub = lax.broadcasted_iota(jnp.int32, (SUBLANES, tn), 0)
    cand, cidx = [], []
    for a in range(SUBLANES):
        lim = PEER_TOPK // (a + 1)
        for b0 in range(0, lim, SUBLANES):
            cv = v1[a] + v2a[b0:b0 + SUBLANES]
            ci = i1[a] * float(nk) + i2a[b0:b0 + SUBLANES]
            if lim - b0 < SUBLANES:
                cv = jnp.where(sub < lim - b0, cv, -jnp.inf)
            cand.append(cv)
            cidx.append(ci)
    cand.append(v1a[SUBLANES:] + v2[0])
    cidx.append(i1a[SUBLANES:] * float(nk) + i2[0])
    cand = jnp.concatenate(cand, axis=0)
    cidx = jnp.concatenate(cidx, axis=0)
    rid = lax.broadcasted_iota(jnp.int32, cand.shape, 0).astype(F32)
    sc, ids = [], []
    for _ in range(PEER_TOPK):
        m = jnp.max(cand, axis=0, keepdims=True)
        am = jnp.min(jnp.where(cand == m, rid, float(cand.shape[0])), axis=0, keepdims=True)
        hit = rid == am
        sc.append(m)
        ids.append(jnp.sum(jnp.where(hit, cidx, 0.0), axis=0, keepdims=True))
        cand = jnp.where(hit, -jnp.inf, cand)
    sca = _stack_rows(sc, tn)
    ida = _stack_rows(ids, tn)
    e = jnp.exp(sca - sc[0])
    wts = e / jnp.sum(e, axis=0, keepdims=True)
    ia = jnp.floor(ida * (1.0 / nk))
    return wts, ia, ida - ia * float(nk)


def _topk_kernel(h_ref, tab_ref, w_ref, a_ref, b_ref, s_scr, w_scr, a_scr, b_scr):
    hd = pl.program_id(1)
    tn = h_ref.shape[0]
    nk = PEER_NKEYS
    s_scr[...] = lax.dot_general(tab_ref[...], h_ref[...], (((1,), (1,)), ((), ())),
                                 preferred_element_type=F32)
    r0 = pl.multiple_of(hd * PEER_TOPK, PEER_TOPK)

    width = min(tn, 2 * LANES)

    def lane_block(c, carry):
        cols = pl.ds(pl.multiple_of(c * width, width), width)
        wts, ia, ib = _topk_columns(s_scr[0:nk, cols], s_scr[nk:2 * nk, cols])
        w_scr[pl.ds(r0, PEER_TOPK), cols] = wts
        a_scr[pl.ds(r0, PEER_TOPK), cols] = ia
        b_scr[pl.ds(r0, PEER_TOPK), cols] = ib
        return carry

    lax.fori_loop(0, tn // width, lane_block, 0)

    @pl.when(hd == pl.num_programs(1) - 1)
    def _():
        w_ref[...] = w_scr[...].T
        a_ref[...] = a_scr[...].T
        b_ref[...] = b_scr[...].T


def _topk(h2, table, tn):
    n, d = h2.shape
    nsel = PEER_HEADS * PEER_TOPK
    out = jax.ShapeDtypeStruct((n, nsel), F32)
    ospec = pl.BlockSpec((tn, nsel), lambda i, h: (i, 0))
    return pl.pallas_call(
        _topk_kernel,
        grid=(n // tn, PEER_HEADS),
        in_specs=[pl.BlockSpec((tn, d), lambda i, h: (i, 0)),
                  pl.BlockSpec((2 * PEER_NKEYS, d), lambda i, h: (h, 0))],
        out_specs=[ospec, ospec, ospec],
        out_shape=[out, out, out],
        scratch_shapes=[pltpu.VMEM((2 * PEER_NKEYS, tn), F32)] + [pltpu.VMEM((nsel, tn), F32)] * 3,
        compiler_params=_params("parallel", "arbitrary"),
        name="topk",
    )(h2, table)


def _act_kernel(h_ref, u_ref, w_ref, a_ref, b_ref, coef_ref, sel_scr, s_scr):
    k = pl.program_id(1)
    nb = u_ref.shape[0] // LANES

    @pl.when(k == 0)
    def _():
        sel_scr[...] = jnp.zeros(sel_scr.shape, F32)
        s_scr[...] = jnp.zeros(s_scr.shape, F32)

    ia = a_ref[...]
    ib = b_ref[...].astype(jnp.int32)

    def gather(key0, sel):
        for j in range(nb):
            g = jnp.take_along_axis(s_scr[:, j * LANES:(j + 1) * LANES], ib, axis=1)
            sel = jnp.where(ia == key0 + float(j), g, sel)
        return sel

    base = (k * nb).astype(F32)
    sel = gather(base - float(nb), sel_scr[...])
    s_scr[...] = lax.dot_general(h_ref[...], u_ref[...].astype(BF16), (((1,), (1,)), ((), ())),
                                 preferred_element_type=F32)
    sel_scr[...] = sel

    @pl.when(k == pl.num_programs(1) - 1)
    def _():
        x = gather(base, sel)
        act = 0.5 * x * (1.0 + lax.erf(x * (2.0 ** -0.5)))
        coef_ref[...] = w_ref[...] * act


def _expert_act(h2, u_bf, wts, ia, ib, tn, te):
    n, d = h2.shape
    ne = u_bf.shape[0]
    nsel = wts.shape[1]
    sspec = pl.BlockSpec((tn, nsel), lambda i, k: (i, 0))
    return pl.pallas_call(
        _act_kernel,
        grid=(n // tn, ne // te),
        in_specs=[pl.BlockSpec((tn, d), lambda i, k: (i, 0)),
                  pl.BlockSpec((te, d), lambda i, k: (k, 0)),
                  sspec, sspec, sspec],
        out_specs=sspec,
        out_shape=jax.ShapeDtypeStruct((n, nsel), F32),
        scratch_shapes=[pltpu.VMEM((tn, nsel), F32), pltpu.VMEM((tn, te), F32)],
        compiler_params=_params("parallel", "arbitrary"),
        name="expert_act",
    )(h2, u_bf, wts, ia, ib)


def _mixmat_kernel(coef_ref, a_ref, b_ref, p_ref, p_scr):
    tn = coef_ref.shape[0]
    nk = PEER_NKEYS
    kid = lax.broadcasted_iota(jnp.int32, (nk, coef_ref.shape[1]), 0).astype(F32).astype(BF16)
    one = jnp.ones((nk, coef_ref.shape[1]), BF16)
    zero = jnp.zeros((nk, coef_ref.shape[1]), BF16)

    unroll = 4 * SUBLANES

    def group(g, carry):
        t0 = pl.multiple_of(g * unroll, unroll)
        for s in range(unroll):
            c = coef_ref[pl.ds(t0 + s, 1), :].astype(BF16)
            ia = a_ref[pl.ds(t0 + s, 1), :].astype(BF16)
            ib = b_ref[pl.ds(t0 + s, 1), :].astype(BF16)
            lhs = jnp.where(ia == kid, c, zero)
            rhs = jnp.where(ib == kid, one, zero)
            row0 = (g * (unroll // SUBLANES) + s // SUBLANES) * (nk * SUBLANES) + s % SUBLANES
            p_scr[pl.ds(row0, nk, stride=SUBLANES), :] = lax.dot_general(
                lhs, rhs, (((1,), (1,)), ((), ())), preferred_element_type=F32)
        return carry

    lax.fori_loop(0, tn // unroll, group, 0)

    def emit(rr, carry):
        lo = pl.multiple_of(rr * (2 * nk * SUBLANES), SUBLANES)
        hi = lo + nk * SUBLANES
        row = pl.multiple_of(rr * 2 * SUBLANES, 2 * SUBLANES)
        for j in range(nk):
            tile = jnp.concatenate([p_scr[pl.ds(lo + j * SUBLANES, SUBLANES), :],
                                    p_scr[pl.ds(hi + j * SUBLANES, SUBLANES), :]], axis=0)
            p_ref[pl.ds(row, 2 * SUBLANES), j * nk:(j + 1) * nk] = tile.astype(p_ref.dtype)
        return carry

    lax.fori_loop(0, tn // (2 * SUBLANES), emit, 0)


def _mixmat(coef, ia, ib, tn):
    n, nsel = coef.shape
    ne = PEER_NKEYS * PEER_NKEYS
    sspec = pl.BlockSpec((tn, nsel), lambda i: (i, 0))
    return pl.pallas_call(
        _mixmat_kernel,
        grid=(n // tn,),
        in_specs=[sspec, sspec, sspec],
        out_specs=pl.BlockSpec((tn, ne), lambda i: (i, 0)),
        out_shape=jax.ShapeDtypeStruct((n, ne), BF16),
        scratch_shapes=[pltpu.VMEM((tn * PEER_NKEYS, PEER_NKEYS), F32)],
        compiler_params=_params("parallel"),
        name="mixmat",
    )(coef, ia, ib)


def _value_kernel(p_ref, v_ref, x1_ref, gt2_ref, g_ref, y_ref, acc_scr):
    k = pl.program_id(2)

    @pl.when(k == 0)
    def _():
        acc_scr[...] = jnp.zeros(acc_scr.shape, F32)

    acc_scr[...] += jnp.dot(p_ref[0], v_ref[...], preferred_element_type=F32)

    @pl.when(k == pl.num_programs(2) - 1)
    def _():
        y_ref[0] = x1_ref[0] + gt2_ref[0] * (_rms(acc_scr[...]) * g_ref[...])


def _value(p, v_bf, x1, gt2, g_post, tt, tk):
    bsz, t, d = x1.shape
    ne = v_bf.shape[0]
    return pl.pallas_call(
        _value_kernel,
        grid=(bsz, t // tt, ne // tk),
        in_specs=[pl.BlockSpec((1, tt, tk), lambda b, i, k: (b, i, k)),
                  pl.BlockSpec((tk, d), lambda b, i, k: (k, 0)),
                  pl.BlockSpec((1, tt, d), lambda b, i, k: (b, i, 0),
                               pipeline_mode=pl.Buffered(1)),
                  _mod_spec(gt2, tt),
                  pl.BlockSpec((1, d), lambda b, i, k: (0, 0))],
        out_specs=pl.BlockSpec((1, tt, d), lambda b, i, k: (b, i, 0),
                               pipeline_mode=pl.Buffered(1)),
        out_shape=jax.ShapeDtypeStruct((bsz, t, d), F32),
        scratch_shapes=[pltpu.VMEM((tt, d), F32)],
        compiler_params=_params("parallel", "parallel", "arbitrary"),
        name="value",
    )(p, v_bf, x1, gt2, g_post)


def _layer(x, mods, s0, cache, wts, *, rec_shape, tt):
    sh1, sc1, gt1, sh2, sc2, gt2 = mods
    bsz, t, d = x.shape
    z, zlr = _inproj(x, sc1, sh1, wts["g_pre_mix"], wts["w_rows"], wts["w_lr"], wts["lr0"], tt)
    rs, rt = rec_shape
    zr = z.reshape(rs, rt, z.shape[-1])
    oa, s_new = _gla(zr, zlr.reshape(rs, rt, LANES), wts["w_au"], wts["b_alpha"],
                     wts["g_gla_out"], s0)
    cb, c_new = _conv(zr, cache, wts["w_dw"], wts["b_dw"], wts["g_conv_ln"], wts["b_conv_ln"],
                      min(rt, 256))
    x1, h2 = _merge(oa.reshape(bsz, t, -1), cb.reshape(bsz, t, -1), z, x, gt1, sc2, sh2,
                    wts["g_post_mix"], wts["g_pre_ffn"], wts["w_gla_o"], wts["w_conv_o"],
                    wts["w_out"], min(t, 256))
    n = bsz * t
    h2f = h2.reshape(n, d)
    pw, ia, ib = _topk(h2f, wts["key_table"], min(n, 512))
    coef = _expert_act(h2f, wts["u_exp"], pw, ia, ib, min(n, 1024), 512)
    p = _mixmat(coef, ia, ib, 128)
    y = _value(p.reshape(bsz, t, -1), wts["v_bf"], x1, gt2, wts["g_post_ffn"], min(t, 512), 2048)
    return y, s_new, c_new


def kernel(x_prompt, x_sample, state_gla, cache_conv, c_prompt, c_sample, w_ada, b_ada, g_pre_mix,
           g_post_mix, w_in, w_alpha_up, b_alpha, g_gla_out, w_gla_o, w_dw, b_dw, g_conv_ln,
           b_conv_ln, w_conv_o, w_out, g_pre_ffn, g_post_ffn, w_pq, k_sub1, k_sub2, u_exp, v_exp):
    depth = w_ada.shape[0]
    bp, tp, d = x_prompt.shape
    bs, ts, _ = x_sample.shape
    nh, dk, dv = state_gla.shape[2:]
    xp, xs = x_prompt, x_sample.reshape(1, bs * ts, d)
    outs = [[], [], [], []]
    for l in range(depth):
        lr0 = 2 * nh * dk + 2 * nh * dv
        wl = w_in[l].T
        row = lambda a: a[l].reshape(1, -1)
        wts = dict(
            w_rows=wl, lr0=lr0,
            w_lr=jnp.pad(wl[lr0:lr0 + GLA_GATE_RANK],
                         ((0, LANES - GLA_GATE_RANK), (0, 0))).astype(BF16),
            w_au=jnp.pad(w_alpha_up[l], ((0, LANES - GLA_GATE_RANK), (0, 0))),
            b_alpha=row(b_alpha), g_gla_out=row(g_gla_out), g_pre_mix=row(g_pre_mix),
            g_post_mix=row(g_post_mix), g_pre_ffn=row(g_pre_ffn), g_post_ffn=row(g_post_ffn),
            w_dw=jnp.broadcast_to(w_dw[l][:, None, :], (CONV_WIDTH, SUBLANES, w_dw.shape[-1])),
            b_dw=row(b_dw), g_conv_ln=row(g_conv_ln), b_conv_ln=row(b_conv_ln),
            w_gla_o=w_gla_o[l].astype(BF16), w_conv_o=w_conv_o[l].astype(BF16),
            w_out=w_out[l].astype(BF16),
            key_table=_keyfold(jnp.stack([k_sub1[l], k_sub2[l]]), w_pq[l]),
            u_exp=u_exp[l], v_bf=v_exp[l].astype(BF16),
        )
        c_all = jnp.concatenate(
            [c_prompt, c_sample, jnp.zeros((-(bp + bs) % SUBLANES, d), F32)], axis=0)
        mod = _ada(c_all, w_ada[l], b_ada[l])
        mods_p = [m.reshape(bp, 1, d) for m in jnp.split(mod[:bp], 6, axis=-1)]
        mods_s = [jnp.repeat(m, ts, axis=0).reshape(1, bs * ts, d)
                  for m in jnp.split(mod[bp:bp + bs], 6, axis=-1)]
        s0_p = jnp.zeros((bp, nh, dk, dv), F32)
        cache0_p = jnp.zeros((bp, CONV_WIDTH - 1, cache_conv.shape[-1]), F32)
        xp, s_p, cb_p = _layer(xp, mods_p, s0_p, cache0_p, wts, rec_shape=(bp, tp), tt=1024)
        xs, s_s, cb_s = _layer(xs, mods_s, state_gla[l], cache_conv[l], wts,
                               rec_shape=(bs, ts), tt=bs * ts)
        for o, v in zip(outs, (s_p, cb_p, s_s, cb_s)):
            o.append(v)
    return (xp, xs.reshape(bs, ts, d), jnp.stack(outs[0]), jnp.stack(outs[1]),
            jnp.stack(outs[2]), jnp.stack(outs[3]))
```

```python
import functools

import jax
import jax.numpy as jnp
from jax import lax
from jax.experimental import pallas as pl
from jax.experimental.pallas import tpu as pltpu

F32 = jnp.float32
BF16 = jnp.bfloat16

CHUNK = 64
GLA_HEADS = 4
GLA_GATE_RANK = 16
GLA_GATE_NORM = 16.0
CONV_WIDTH = 31
PEER_HEADS = 8
PEER_NKEYS = 128
PEER_TOPK = 16
EPS = 1e-6

LANES = 128
SUBLANES = 8
MXU_WIDTH = 256
CONV_HALO = 32
VMEM_LIMIT = 52 * 1024 * 1024


def _params(*sem):
    return pltpu.CompilerParams(dimension_semantics=sem, vmem_limit_bytes=VMEM_LIMIT)


def _const_spec(shape):
    nd = len(shape)
    return pl.BlockSpec(shape, lambda *_: (0,) * nd, pipeline_mode=pl.Buffered(1))


def _rms(x):
    return x * lax.rsqrt(jnp.mean(x * x, axis=-1, keepdims=True) + EPS)


def _sigmoid(x):
    return 1.0 / (1.0 + jnp.exp(-x))


def _silu(x):
    return x * _sigmoid(x)


def _ada_kernel(c_ref, w_ref, b_ref, o_ref):
    c = c_ref[...]
    o_ref[...] = jnp.dot(_silu(c), w_ref[...], preferred_element_type=F32,
                         precision=lax.Precision.HIGHEST) + b_ref[...]


def _ada(c, w_ada, b_ada):
    g, d = c.shape
    n = w_ada.shape[1]
    tc = 1024
    return pl.pallas_call(
        _ada_kernel,
        grid=(n // tc,),
        in_specs=[pl.BlockSpec((g, d), lambda j: (0, 0)),
                  pl.BlockSpec((d, tc), lambda j: (0, j)),
                  pl.BlockSpec((1, tc), lambda j: (0, j))],
        out_specs=pl.BlockSpec((g, tc), lambda j: (0, j)),
        out_shape=jax.ShapeDtypeStruct((g, n), F32),
        compiler_params=_params("arbitrary"),
        name="ada",
    )(c, w_ada, b_ada.reshape(1, n))


def _inproj_kernel(x_ref, sc_ref, sh_ref, g_ref, w_ref, wlr_ref, z_ref, zlr_ref, h_scr):
    @pl.when(pl.program_id(2) == 0)
    def _():
        h = _rms(x_ref[0]) * g_ref[...]
        h = (h * (1.0 + sc_ref[0]) + sh_ref[0]).astype(BF16)
        h_scr[...] = h
        zlr_ref[0] = lax.dot_general(h, wlr_ref[...], (((1,), (1,)), ((), ())),
                                     preferred_element_type=F32)

    z_ref[0] = lax.dot_general(h_scr[...], w_ref[...].astype(BF16), (((1,), (1,)), ((), ())),
                               preferred_element_type=F32)


def _mod_spec(m, tt):
    d = m.shape[-1]
    if m.shape[1] == 1:
        return pl.BlockSpec((1, 1, d), lambda b, i, *_: (b, 0, 0))
    return pl.BlockSpec((1, tt, d), lambda b, i, *_: (b, i, 0))


def _inproj(x, sc, sh, g, w_rows, w_lr, lr0, tt):
    bsz, t, d = x.shape
    n = w_rows.shape[0] - GLA_GATE_RANK
    tc = 1024
    assert lr0 % tc == 0 and n == 2 * lr0
    wrow = lambda b, i, j: (
        pl.multiple_of(j * tc + (j // (lr0 // tc)) * GLA_GATE_RANK, GLA_GATE_RANK), 0)
    return pl.pallas_call(
        _inproj_kernel,
        grid=(bsz, t // tt, n // tc),
        in_specs=[pl.BlockSpec((1, tt, d), lambda b, i, j: (b, i, 0)),
                  _mod_spec(sc, tt), _mod_spec(sh, tt),
                  pl.BlockSpec((1, d), lambda b, i, j: (0, 0)),
                  pl.BlockSpec((pl.Element(tc), pl.Element(d)), wrow),
                  pl.BlockSpec((LANES, d), lambda b, i, j: (0, 0))],
        out_specs=[pl.BlockSpec((1, tt, tc), lambda b, i, j: (b, i, j)),
                   pl.BlockSpec((1, tt, LANES), lambda b, i, j: (b, i, 0))],
        out_shape=[jax.ShapeDtypeStruct((bsz, t, n), F32),
                   jax.ShapeDtypeStruct((bsz, t, LANES), F32)],
        scratch_shapes=[pltpu.VMEM((tt, d), BF16)],
        compiler_params=_params("parallel", "parallel", "arbitrary"),
        name="inproj",
    )(x, sc, sh, g, w_rows, w_lr)


def _gla_kernel(q_ref, k_ref, v_ref, r_ref, lr_ref, wau_ref, ba_ref, gg_ref, s0_ref,
                o_ref, s_ref, st_scr, *, dk, dv):
    c = pl.program_id(1)
    streams, rows = q_ref.shape[0], q_ref.shape[1]

    @pl.when(c == 0)
    def _():
        for s in range(streams):
            for h in range(GLA_HEADS):
                st_scr[s, h] = s0_ref[s, h].T

    for s in range(streams):
        _gla_chunk(s, q_ref, k_ref, v_ref, r_ref, lr_ref, wau_ref, ba_ref, gg_ref, o_ref, st_scr,
                   dk=dk, dv=dv)

    @pl.when(c == pl.num_programs(1) - 1)
    def _():
        for s in range(streams):
            for h in range(GLA_HEADS):
                s_ref[s, h] = st_scr[s, h].T


def _gla_chunk(s, q_ref, k_ref, v_ref, r_ref, lr_ref, wau_ref, ba_ref, gg_ref, o_ref, st_scr,
               *, dk, dv):
    rows = q_ref.shape[1]
    pre = jnp.dot(lr_ref[s].astype(BF16), wau_ref[...], preferred_element_type=F32) + ba_ref[...]
    log_a = (jnp.minimum(pre, 0.0) - jnp.log1p(jnp.exp(-jnp.abs(pre)))) * (1.0 / GLA_GATE_NORM)
    row = lax.broadcasted_iota(jnp.int32, log_a.shape, 0)
    b = log_a
    shift = 1
    while shift < rows:
        b = b + jnp.where(row >= shift, pltpu.roll(b, shift, axis=0), 0.0)
        shift *= 2
    b_last = b[rows - 1:rows, :]
    e_pos = jnp.exp(b)
    q_in = q_ref[s] * (dk ** -0.5) * e_pos
    kk = k_ref[s]
    k_in = kk * jnp.exp(-b)
    k_out = kk * jnp.exp(b_last - b)
    e_last = jnp.exp(b_last)
    causal = (lax.broadcasted_iota(jnp.int32, (rows, rows), 0)
              >= lax.broadcasted_iota(jnp.int32, (rows, rows), 1))
    gg = gg_ref[...]
    for h in range(GLA_HEADS):
        ks = slice(h * dk, (h + 1) * dk)
        vs = slice(h * dv, (h + 1) * dv)
        qh = q_in[:, ks].astype(BF16)
        vh = v_ref[s, :, vs].astype(BF16)
        st = st_scr[s, h]
        att = lax.dot_general(qh, k_in[:, ks].astype(BF16), (((1,), (1,)), ((), ())),
                              preferred_element_type=F32)
        att = jnp.where(causal, att, 0.0).astype(BF16)
        o = jnp.dot(att, vh, preferred_element_type=F32)
        o = o + lax.dot_general(qh, st.astype(BF16), (((1,), (1,)), ((), ())),
                                preferred_element_type=F32)
        st_scr[s, h] = st * e_last[:, ks] + lax.dot_general(
            vh, k_out[:, ks].astype(BF16), (((0,), (0,)), ((), ())), preferred_element_type=F32)
        o = _rms(o) * gg * _silu(r_ref[s, :, vs])
        o_ref[s, :, vs] = o.astype(o_ref.dtype)


def _gla(z, zlr, w_au, b_alpha, g_gla_out, s0):
    bsz, t, _ = z.shape
    _, nh, dk, dv = s0.shape
    rows = min(CHUNK, t)
    sb = 2 if bsz % 2 == 0 else 1
    kern = functools.partial(_gla_kernel, dk=dk, dv=dv)
    return pl.pallas_call(
        kern,
        grid=(bsz // sb, t // rows),
        in_specs=[pl.BlockSpec((sb, rows, nh * dk), lambda b, c: (b, c, 0)),
                  pl.BlockSpec((sb, rows, nh * dk), lambda b, c: (b, c, 1)),
                  pl.BlockSpec((sb, rows, nh * dv), lambda b, c: (b, c, 1)),
                  pl.BlockSpec((sb, rows, nh * dv), lambda b, c: (b, c, 2)),
                  pl.BlockSpec((sb, rows, LANES), lambda b, c: (b, c, 0)),
                  pl.BlockSpec((LANES, nh * dk), lambda b, c: (0, 0)),
                  pl.BlockSpec((1, nh * dk), lambda b, c: (0, 0)),
                  pl.BlockSpec((1, dv), lambda b, c: (0, 0)),
                  pl.BlockSpec((sb, nh, dk, dv), lambda b, c: (b, 0, 0, 0))],
        out_specs=[pl.BlockSpec((sb, rows, nh * dv), lambda b, c: (b, c, 0)),
                   pl.BlockSpec((sb, nh, dk, dv), lambda b, c: (b, 0, 0, 0))],
        out_shape=[jax.ShapeDtypeStruct((bsz, t, nh * dv), BF16),
                   jax.ShapeDtypeStruct((bsz, nh, dk, dv), F32)],
        scratch_shapes=[pltpu.VMEM((sb, nh, dv, dk), F32)],
        compiler_params=_params("parallel", "arbitrary"),
        name="gla",
    )(z, z, z, z, zlr, w_au, b_alpha, g_gla_out, s0)


def _conv_kernel(a_ref, g_ref, cache_ref, wdw_ref, bdw_ref, gln_ref, bln_ref,
                 y_ref, newc_ref, shift_scr, conv_scr):
    i = pl.program_id(1)
    tt = a_ref.shape[1]
    hist = CONV_WIDTH - 1
    pad = CONV_HALO - hist
    win_scr = shift_scr.at[0]

    @pl.when(i == 0)
    def _():
        win_scr[0:pad, :] = jnp.zeros((pad, win_scr.shape[1]), F32)
        win_scr[pad:CONV_HALO, :] = cache_ref[0]

    win_scr[CONV_HALO:CONV_HALO + tt, :] = a_ref[0] * _sigmoid(g_ref[0])
    full = win_scr[...]
    for s in range(1, SUBLANES):
        shift_scr[s] = pltpu.roll(full, full.shape[0] - s, axis=0)
    bdw = bdw_ref[...]
    gln = gln_ref[...]
    bln = bln_ref[...]

    rows = 2 * SUBLANES

    def chunk(ci, carry):
        r0 = pl.multiple_of(ci * rows, rows)
        acc = jnp.zeros((2, SUBLANES, win_scr.shape[1]), F32)
        for w in range(CONV_WIDTH):
            q, s = divmod(pad + w, SUBLANES)
            x = shift_scr[s, pl.ds(r0 + q * SUBLANES, rows), :]
            acc = acc + x.reshape(2, SUBLANES, x.shape[1]) * wdw_ref[w]
        conv_scr[pl.ds(r0, rows), :] = acc.reshape(rows, win_scr.shape[1])
        return carry

    lax.fori_loop(0, tt // rows, chunk, 0)
    acc = conv_scr[...] + bdw
    mu = jnp.mean(acc, axis=-1, keepdims=True)
    xc = acc - mu
    yn = xc * lax.rsqrt(jnp.mean(xc * xc, axis=-1, keepdims=True) + EPS) * gln + bln
    y_ref[0] = _silu(yn).astype(y_ref.dtype)

    @pl.when(i == pl.num_programs(1) - 1)
    def _():
        newc_ref[0] = win_scr[tt + pad:tt + CONV_HALO, :]

    tail = win_scr[tt:tt + CONV_HALO, :]
    win_scr[0:CONV_HALO, :] = tail


def _conv(z, cache, w_dw, b_dw, g_ln, b_ln, tt):
    bsz, t, _ = z.shape
    _, hist, ch = cache.shape
    return pl.pallas_call(
        _conv_kernel,
        grid=(bsz, t // tt),
        in_specs=[pl.BlockSpec((1, tt, ch), lambda b, i: (b, i, 6)),
                  pl.BlockSpec((1, tt, ch), lambda b, i: (b, i, 7)),
                  pl.BlockSpec((1, hist, ch), lambda b, i: (b, 0, 0)),
                  pl.BlockSpec((CONV_WIDTH, SUBLANES, ch), lambda b, i: (0, 0, 0)),
                  pl.BlockSpec((1, ch), lambda b, i: (0, 0)),
                  pl.BlockSpec((1, ch), lambda b, i: (0, 0)),
                  pl.BlockSpec((1, ch), lambda b, i: (0, 0))],
        out_specs=[pl.BlockSpec((1, tt, ch), lambda b, i: (b, i, 0)),
                   pl.BlockSpec((1, hist, ch), lambda b, i: (b, 0, 0))],
        out_shape=[jax.ShapeDtypeStruct((bsz, t, ch), BF16),
                   jax.ShapeDtypeStruct((bsz, hist, ch), F32)],
        scratch_shapes=[pltpu.VMEM((SUBLANES, tt + CONV_HALO, ch), F32), pltpu.VMEM((tt, ch), F32)],
        compiler_params=_params("parallel", "arbitrary"),
        name="conv",
    )(z, z, cache, w_dw, b_dw, g_ln, b_ln)


def _merge_kernel(oa_ref, cb_ref, ga_ref, gb_ref, x_ref, gt1_ref, sc2_ref, sh2_ref,
                  gpost_ref, gpre_ref, wgo_ref, wco_ref, wout_ref, x1_ref, h2_ref):
    ya = jnp.dot(oa_ref[0], wgo_ref[...], preferred_element_type=F32)
    yb = jnp.dot(cb_ref[0], wco_ref[...], preferred_element_type=F32)
    m = _sigmoid(ga_ref[0]) * ya + _sigmoid(gb_ref[0]) * yb
    mix = jnp.dot(m.astype(BF16), wout_ref[...], preferred_element_type=F32)
    x1 = x_ref[0] + gt1_ref[0] * (_rms(mix) * gpost_ref[...])
    x1_ref[0] = x1
    h2 = _rms(x1) * gpre_ref[...]
    h2_ref[0] = (h2 * (1.0 + sc2_ref[0]) + sh2_ref[0]).astype(h2_ref.dtype)


def _merge(oa, cb, z, x, gt1, sc2, sh2, g_post, g_pre, w_go, w_co, w_out, tt):
    bsz, t, d = x.shape
    ch = cb.shape[-1]
    tok = lambda w: pl.BlockSpec((1, tt, w), lambda b, i: (b, i, 0))
    return pl.pallas_call(
        _merge_kernel,
        grid=(bsz, t // tt),
        in_specs=[tok(d), tok(ch),
                  pl.BlockSpec((1, tt, d), lambda b, i: (b, i, 4)),
                  pl.BlockSpec((1, tt, d), lambda b, i: (b, i, 5)),
                  tok(d), _mod_spec(gt1, tt), _mod_spec(sc2, tt), _mod_spec(sh2, tt),
                  _const_spec((1, d)), _const_spec((1, d)),
                  _const_spec(w_go.shape), _const_spec(w_co.shape), _const_spec(w_out.shape)],
        out_specs=[tok(d), tok(d)],
        out_shape=[jax.ShapeDtypeStruct((bsz, t, d), F32),
                   jax.ShapeDtypeStruct((bsz, t, d), BF16)],
        compiler_params=_params("parallel", "parallel"),
        name="merge",
    )(oa, cb, z, z, x, gt1, sc2, sh2, g_post, g_pre, w_go, w_co, w_out)


def _keyfold_kernel(ks_ref, wq_ref, o_ref):
    o_ref[...] = lax.dot_general(ks_ref[0], wq_ref[...], (((1,), (1,)), ((), ())),
                                 preferred_element_type=F32,
                                 precision=lax.Precision.HIGHEST).astype(o_ref.dtype)


def _keyfold(k_sub, w_pq):
    d = w_pq.shape[0]
    _, nk, dq = k_sub.shape
    nblk = w_pq.shape[1] // dq
    return pl.pallas_call(
        _keyfold_kernel,
        grid=(nblk,),
        in_specs=[pl.BlockSpec((1, nk, dq), lambda j: (j % 2, 0, 0)),
                  pl.BlockSpec((d, dq), lambda j: (0, j))],
        out_specs=pl.BlockSpec((nk, d), lambda j: (j, 0)),
        out_shape=jax.ShapeDtypeStruct((nblk * nk, d), BF16),
        compiler_params=_params("arbitrary"),
        name="keyfold",
    )(k_sub, w_pq)


def _top16(s):
    rows = s.shape[0]
    rid = lax.broadcasted_iota(jnp.int32, s.shape, 0).astype(F32)
    vals, idxs = [], []
    for _ in range(PEER_TOPK):
        m = jnp.max(s, axis=0, keepdims=True)
        am = jnp.min(jnp.where(s == m, rid, float(rows)), axis=0, keepdims=True)
        vals.append(m)
        idxs.append(am)
        s = jnp.where(rid == am, -jnp.inf, s)
    return vals, idxs


def _stack_rows(rows_list, n):
    k = len(rows_list)
    rid = lax.broadcasted_iota(jnp.int32, (k, n), 0)
    out = jnp.zeros((k, n), F32)
    for r, v in enumerate(rows_list):
        out = jnp.where(rid == r, v, out)
    return out


def _topk_columns(s1, s2):
    tn = s1.shape[1]
    nk = PEER_NKEYS
    v1, i1 = _top16(s1)
    v2, i2 = _top16(s2)
    v2a = _stack_rows(v2, tn)
    i2a = _stack_rows(i2, tn)
    v1a = _stack_rows(v1, tn)
    i1a = _stack_rows(i1, tn)
    sub = lax.broadcasted_iota(jnp.int32, (SUBLANES, tn), 0)
    cand, cidx = [], []
    for a in range(SUBLANES):
        lim = PEER_TOPK // (a + 1)
        for b0 in range(0, lim, SUBLANES):
            cv = v1[a] + v2a[b0:b0 + SUBLANES]
            ci = i1[a] * float(nk) + i2a[b0:b0 + SUBLANES]
            if lim - b0 < SUBLANES:
                cv = jnp.where(sub < lim - b0, cv, -jnp.inf)
            cand.append(cv)
            cidx.append(ci)
    cand.append(v1a[SUBLANES:] + v2[0])
    cidx.append(i1a[SUBLANES:] * float(nk) + i2[0])
    cand = jnp.concatenate(cand, axis=0)
    cidx = jnp.concatenate(cidx, axis=0)
    rid = lax.broadcasted_iota(jnp.int32, cand.shape, 0).astype(F32)
    sc, ids = [], []
    for _ in range(PEER_TOPK):
        m = jnp.max(cand, axis=0, keepdims=True)
        am = jnp.min(jnp.where(cand == m, rid, float(cand.shape[0])), axis=0, keepdims=True)
        hit = rid == am
        sc.append(m)
        ids.append(jnp.sum(jnp.where(hit, cidx, 0.0), axis=0, keepdims=True))
        cand = jnp.where(hit, -jnp.inf, cand)
    sca = _stack_rows(sc, tn)
    ida = _stack_rows(ids, tn)
    e = jnp.exp(sca - sc[0])
    wts = e / jnp.sum(e, axis=0, keepdims=True)
    ia = jnp.floor(ida * (1.0 / nk))
    return wts, ia, ida - ia * float(nk)


def _topk_kernel(h_ref, tab_ref, w_ref, a_ref, b_ref, s_scr, w_scr, a_scr, b_scr):
    hd = pl.program_id(1)
    tn = h_ref.shape[0]
    nk = PEER_NKEYS
    s_scr[...] = lax.dot_general(tab_ref[...], h_ref[...], (((1,), (1,)), ((), ())),
                                 preferred_element_type=F32)
    r0 = pl.multiple_of(hd * PEER_TOPK, PEER_TOPK)

    width = min(tn, 2 * LANES)

    def lane_block(c, carry):
        cols = pl.ds(pl.multiple_of(c * width, width), width)
        wts, ia, ib = _topk_columns(s_scr[0:nk, cols], s_scr[nk:2 * nk, cols])
        w_scr[pl.ds(r0, PEER_TOPK), cols] = wts
        a_scr[pl.ds(r0, PEER_TOPK), cols] = ia
        b_scr[pl.ds(r0, PEER_TOPK), cols] = ib
        return carry

    lax.fori_loop(0, tn // width, lane_block, 0)

    @pl.when(hd == pl.num_programs(1) - 1)
    def _():
        w_ref[...] = w_scr[...].T
        a_ref[...] = a_scr[...].T
        b_ref[...] = b_scr[...].T


def _topk(h2, table, tn):
    n, d = h2.shape
    nsel = PEER_HEADS * PEER_TOPK
    out = jax.ShapeDtypeStruct((n, nsel), F32)
    ospec = pl.BlockSpec((tn, nsel), lambda i, h: (i, 0))
    return pl.pallas_call(
        _topk_kernel,
        grid=(n // tn, PEER_HEADS),
        in_specs=[pl.BlockSpec((tn, d), lambda i, h: (i, 0)),
                  pl.BlockSpec((2 * PEER_NKEYS, d), lambda i, h: (h, 0))],
        out_specs=[ospec, ospec, ospec],
        out_shape=[out, out, out],
        scratch_shapes=[pltpu.VMEM((2 * PEER_NKEYS, tn), F32)] + [pltpu.VMEM((nsel, tn), F32)] * 3,
        compiler_params=_params("parallel", "arbitrary"),
        name="topk",
    )(h2, table)


def _act_kernel(h_ref, u_ref, w_ref, a_ref, b_ref, coef_ref, sel_scr, s_scr):
    k = pl.program_id(1)
    nb = u_ref.shape[0] // LANES

    @pl.when(k == 0)
    def _():
        sel_scr[...] = jnp.zeros(sel_scr.shape, F32)
        s_scr[...] = jnp.zeros(s_scr.shape, F32)

    ia = a_ref[...]
    ib = b_ref[...].astype(jnp.int32)

    def gather(key0, sel):
        for j in range(nb):
            g = jnp.take_along_axis(s_scr[:, j * LANES:(j + 1) * LANES], ib, axis=1)
            sel = jnp.where(ia == key0 + float(j), g, sel)
        return sel

    base = (k * nb).astype(F32)
    sel = gather(base - float(nb), sel_scr[...])
    s_scr[...] = lax.dot_general(h_ref[...], u_ref[...].astype(BF16), (((1,), (1,)), ((), ())),
                                 preferred_element_type=F32)
    sel_scr[...] = sel

    @pl.when(k == pl.num_programs(1) - 1)
    def _():
        x = gather(base, sel)
        act = 0.5 * x * (1.0 + lax.erf(x * (2.0 ** -0.5)))
        coef_ref[...] = w_ref[...] * act


def _expert_act(h2, u_bf, wts, ia, ib, tn, te):
    n, d = h2.shape
    ne = u_bf.shape[0]
    nsel = wts.shape[1]
    sspec = pl.BlockSpec((tn, nsel), lambda i, k: (i, 0))
    return pl.pallas_call(
        _act_kernel,
        grid=(n // tn, ne // te),
        in_specs=[pl.BlockSpec((tn, d), lambda i, k: (i, 0)),
                  pl.BlockSpec((te, d), lambda i, k: (k, 0)),
                  sspec, sspec, sspec],
        out_specs=sspec,
        out_shape=jax.ShapeDtypeStruct((n, nsel), F32),
        scratch_shapes=[pltpu.VMEM((tn, nsel), F32), pltpu.VMEM((tn, te), F32)],
        compiler_params=_params("parallel", "arbitrary"),
        name="expert_act",
    )(h2, u_bf, wts, ia, ib)


def _mixmat_kernel(*refs, cast_values):
    if cast_values:
        coef_ref, a_ref, b_ref, v_ref, p_ref, vb_ref, p_scr = refs
        vb_ref[...] = v_ref[...].astype(vb_ref.dtype)
    else:
        coef_ref, a_ref, b_ref, p_ref, p_scr = refs
    tn = coef_ref.shape[0]
    nk = PEER_NKEYS
    kid = lax.broadcasted_iota(jnp.int32, (nk, coef_ref.shape[1]), 0).astype(F32).astype(BF16)
    one = jnp.ones((nk, coef_ref.shape[1]), BF16)
    zero = jnp.zeros((nk, coef_ref.shape[1]), BF16)

    unroll = 4 * SUBLANES

    def group(g, carry):
        t0 = pl.multiple_of(g * unroll, unroll)
        for s in range(unroll):
            c = coef_ref[pl.ds(t0 + s, 1), :].astype(BF16)
            ia = a_ref[pl.ds(t0 + s, 1), :].astype(BF16)
            ib = b_ref[pl.ds(t0 + s, 1), :].astype(BF16)
            lhs = jnp.where(ia == kid, c, zero)
            rhs = jnp.where(ib == kid, one, zero)
            row0 = (g * (unroll // SUBLANES) + s // SUBLANES) * (nk * SUBLANES) + s % SUBLANES
            p_scr[pl.ds(row0, nk, stride=SUBLANES), :] = lax.dot_general(
                lhs, rhs, (((1,), (1,)), ((), ())), preferred_element_type=F32)
        return carry

    lax.fori_loop(0, tn // unroll, group, 0)

    def emit(rr, carry):
        lo = pl.multiple_of(rr * (2 * nk * SUBLANES), SUBLANES)
        hi = lo + nk * SUBLANES
        row = pl.multiple_of(rr * 2 * SUBLANES, 2 * SUBLANES)
        for j in range(nk):
            tile = jnp.concatenate([p_scr[pl.ds(lo + j * SUBLANES, SUBLANES), :],
                                    p_scr[pl.ds(hi + j * SUBLANES, SUBLANES), :]], axis=0)
            p_ref[pl.ds(row, 2 * SUBLANES), j * nk:(j + 1) * nk] = tile.astype(p_ref.dtype)
        return carry

    lax.fori_loop(0, tn // (2 * SUBLANES), emit, 0)


def _mixmat(coef, ia, ib, tn, v_exp=None):
    n, nsel = coef.shape
    ne = PEER_NKEYS * PEER_NKEYS
    steps = n // tn
    sspec = pl.BlockSpec((tn, nsel), lambda i: (i, 0))
    in_specs, out_specs = [sspec, sspec, sspec], [pl.BlockSpec((tn, ne), lambda i: (i, 0))]
    out_shape, args = [jax.ShapeDtypeStruct((n, ne), BF16)], [coef, ia, ib]
    if v_exp is not None:
        d = v_exp.shape[1]
        vspec = pl.BlockSpec((ne // steps, d), lambda i: (i, 0))
        in_specs.append(vspec)
        out_specs.append(vspec)
        out_shape.append(jax.ShapeDtypeStruct((ne, d), BF16))
        args.append(v_exp)
    out = pl.pallas_call(
        functools.partial(_mixmat_kernel, cast_values=v_exp is not None),
        grid=(steps,),
        in_specs=in_specs,
        out_specs=out_specs,
        out_shape=out_shape,
        scratch_shapes=[pltpu.VMEM((tn * PEER_NKEYS, PEER_NKEYS), F32)],
        compiler_params=_params("parallel"),
        name="mixmat",
    )(*args)
    return out if v_exp is not None else (out[0], None)


def _value_kernel(p_ref, v_ref, x1_ref, gt2_ref, g_ref, y_ref, acc_scr):
    k = pl.program_id(2)

    @pl.when(k == 0)
    def _():
        acc_scr[...] = jnp.zeros(acc_scr.shape, F32)

    acc_scr[...] += jnp.dot(p_ref[0], v_ref[...], preferred_element_type=F32)

    @pl.when(k == pl.num_programs(2) - 1)
    def _():
        y_ref[0] = x1_ref[0] + gt2_ref[0] * (_rms(acc_scr[...]) * g_ref[...])


def _value(p, v_bf, x1, gt2, g_post, tt, tk):
    bsz, t, d = x1.shape
    ne = v_bf.shape[0]
    return pl.pallas_call(
        _value_kernel,
        grid=(bsz, t // tt, ne // tk),
        in_specs=[pl.BlockSpec((1, tt, tk), lambda b, i, k: (b, i, k)),
                  pl.BlockSpec((tk, d), lambda b, i, k: (k, 0)),
                  pl.BlockSpec((1, tt, d), lambda b, i, k: (b, i, 0),
                               pipeline_mode=pl.Buffered(1)),
                  _mod_spec(gt2, tt),
                  pl.BlockSpec((1, d), lambda b, i, k: (0, 0))],
        out_specs=pl.BlockSpec((1, tt, d), lambda b, i, k: (b, i, 0),
                               pipeline_mode=pl.Buffered(1)),
        out_shape=jax.ShapeDtypeStruct((bsz, t, d), F32),
        scratch_shapes=[pltpu.VMEM((tt, d), F32)],
        compiler_params=_params("parallel", "parallel", "arbitrary"),
        name="value",
    )(p, v_bf, x1, gt2, g_post)


def _layer(x, mods, s0, cache, wts, *, rec_shape, tt, v_bf=None):
    sh1, sc1, gt1, sh2, sc2, gt2 = mods
    bsz, t, d = x.shape
    z, zlr = _inproj(x, sc1, sh1, wts["g_pre_mix"], wts["w_rows"], wts["w_lr"], wts["lr0"], tt)
    rs, rt = rec_shape
    zr = z.reshape(rs, rt, z.shape[-1])
    oa, s_new = _gla(zr, zlr.reshape(rs, rt, LANES), wts["w_au"], wts["b_alpha"],
                     wts["g_gla_out"], s0)
    cb, c_new = _conv(zr, cache, wts["w_dw"], wts["b_dw"], wts["g_conv_ln"], wts["b_conv_ln"],
                      min(rt, 256))
    x1, h2 = _merge(oa.reshape(bsz, t, -1), cb.reshape(bsz, t, -1), z, x, gt1, sc2, sh2,
                    wts["g_post_mix"], wts["g_pre_ffn"], wts["w_gla_o"], wts["w_conv_o"],
                    wts["w_out"], min(t, 256))
    n = bsz * t
    h2f = h2.reshape(n, d)
    pw, ia, ib = _topk(h2f, wts["key_table"], min(n, 512))
    coef = _expert_act(h2f, wts["u_exp"], pw, ia, ib, min(n, 1024), 1024)
    p, v_new = _mixmat(coef, ia, ib, 128, None if v_bf is not None else wts["v_exp"])
    v_bf = v_bf if v_bf is not None else v_new
    y = _value(p.reshape(bsz, t, -1), v_bf, x1, gt2, wts["g_post_ffn"], min(t, 512), 2048)
    return y, s_new, c_new, v_bf


def kernel(x_prompt, x_sample, state_gla, cache_conv, c_prompt, c_sample, w_ada, b_ada, g_pre_mix,
           g_post_mix, w_in, w_alpha_up, b_alpha, g_gla_out, w_gla_o, w_dw, b_dw, g_conv_ln,
           b_conv_ln, w_conv_o, w_out, g_pre_ffn, g_post_ffn, w_pq, k_sub1, k_sub2, u_exp, v_exp):
    depth = w_ada.shape[0]
    bp, tp, d = x_prompt.shape
    bs, ts, _ = x_sample.shape
    nh, dk, dv = state_gla.shape[2:]
    xp, xs = x_prompt, x_sample.reshape(1, bs * ts, d)
    outs = [[], [], [], []]
    for l in range(depth):
        lr0 = 2 * nh * dk + 2 * nh * dv
        wl = w_in[l].T
        row = lambda a: a[l].reshape(1, -1)
        wts = dict(
            w_rows=wl, lr0=lr0,
            w_lr=jnp.pad(wl[lr0:lr0 + GLA_GATE_RANK],
                         ((0, LANES - GLA_GATE_RANK), (0, 0))).astype(BF16),
            w_au=jnp.pad(w_alpha_up[l], ((0, LANES - GLA_GATE_RANK), (0, 0))).astype(BF16),
            b_alpha=row(b_alpha), g_gla_out=row(g_gla_out), g_pre_mix=row(g_pre_mix),
            g_post_mix=row(g_post_mix), g_pre_ffn=row(g_pre_ffn), g_post_ffn=row(g_post_ffn),
            w_dw=jnp.broadcast_to(w_dw[l][:, None, :], (CONV_WIDTH, SUBLANES, w_dw.shape[-1])),
            b_dw=row(b_dw), g_conv_ln=row(g_conv_ln), b_conv_ln=row(b_conv_ln),
            w_gla_o=w_gla_o[l].astype(BF16), w_conv_o=w_conv_o[l].astype(BF16),
            w_out=w_out[l].astype(BF16),
            key_table=_keyfold(jnp.stack([k_sub1[l], k_sub2[l]]), w_pq[l]),
            u_exp=u_exp[l], v_exp=v_exp[l],
        )
        c_all = jnp.concatenate(
            [c_prompt, c_sample, jnp.zeros((-(bp + bs) % SUBLANES, d), F32)], axis=0)
        mod = _ada(c_all, w_ada[l], b_ada[l])
        mods_p = [m.reshape(bp, 1, d) for m in jnp.split(mod[:bp], 6, axis=-1)]
        mods_s = [jnp.repeat(m, ts, axis=0).reshape(1, bs * ts, d)
                  for m in jnp.split(mod[bp:bp + bs], 6, axis=-1)]
        s0_p = jnp.zeros((bp, nh, dk, dv), F32)
        cache0_p = jnp.zeros((bp, CONV_WIDTH - 1, cache_conv.shape[-1]), F32)
        xp, s_p, cb_p, v_bf = _layer(xp, mods_p, s0_p, cache0_p, wts, rec_shape=(bp, tp), tt=1024)
        xs, s_s, cb_s, _ = _layer(xs, mods_s, state_gla[l], cache_conv[l], wts,
                                  rec_shape=(bs, ts), tt=bs * ts, v_bf=v_bf)
        for o, v in zip(outs, (s_p, cb_p, s_s, cb_s)):
            o.append(v)
    return (xp, xs.reshape(bs, ts, d), jnp.stack(outs[0]), jnp.stack(outs[1]),
            jnp.stack(outs[2]), jnp.stack(outs[3]))
```

```python
import functools

import jax
import jax.numpy as jnp
from jax import lax
from jax.experimental import pallas as pl
from jax.experimental.pallas import tpu as pltpu

F32 = jnp.float32
BF16 = jnp.bfloat16

CHUNK = 64
GLA_HEADS = 4
GLA_GATE_RANK = 16
GLA_GATE_NORM = 16.0
CONV_WIDTH = 31
PEER_HEADS = 8
PEER_NKEYS = 128
PEER_TOPK = 16
EPS = 1e-6

LANES = 128
SUBLANES = 8
MXU_WIDTH = 256
CONV_HALO = 32
VMEM_LIMIT = 52 * 1024 * 1024


def _params(*sem):
    return pltpu.CompilerParams(dimension_semantics=sem, vmem_limit_bytes=VMEM_LIMIT)


def _const_spec(shape):
    nd = len(shape)
    return pl.BlockSpec(shape, lambda *_: (0,) * nd, pipeline_mode=pl.Buffered(1))


def _rms(x):
    return x * lax.rsqrt(jnp.mean(x * x, axis=-1, keepdims=True) + EPS)


def _sigmoid(x):
    return 1.0 / (1.0 + jnp.exp(-x))


def _silu(x):
    return x * _sigmoid(x)


def _ada_kernel(c_ref, w_ref, b_ref, o_ref):
    c = c_ref[...]
    o_ref[...] = jnp.dot(_silu(c), w_ref[...], preferred_element_type=F32,
                         precision=lax.Precision.HIGHEST) + b_ref[...]


def _ada(c, w_ada, b_ada):
    g, d = c.shape
    n = w_ada.shape[1]
    tc = 1024
    return pl.pallas_call(
        _ada_kernel,
        grid=(n // tc,),
        in_specs=[pl.BlockSpec((g, d), lambda j: (0, 0)),
                  pl.BlockSpec((d, tc), lambda j: (0, j)),
                  pl.BlockSpec((1, tc), lambda j: (0, j))],
        out_specs=pl.BlockSpec((g, tc), lambda j: (0, j)),
        out_shape=jax.ShapeDtypeStruct((g, n), F32),
        compiler_params=_params("arbitrary"),
        name="ada",
    )(c, w_ada, b_ada.reshape(1, n))


def _inproj_kernel(x_ref, sc_ref, sh_ref, g_ref, w_ref, wlr_ref, z_ref, zlr_ref, h_scr):
    @pl.when(pl.program_id(2) == 0)
    def _():
        h = _rms(x_ref[0]) * g_ref[...]
        h = (h * (1.0 + sc_ref[0]) + sh_ref[0]).astype(BF16)
        h_scr[...] = h
        zlr_ref[0] = lax.dot_general(h, wlr_ref[...], (((1,), (1,)), ((), ())),
                                     preferred_element_type=F32)

    z_ref[0] = lax.dot_general(h_scr[...], w_ref[...].astype(BF16), (((1,), (1,)), ((), ())),
                               preferred_element_type=F32)


def _mod_spec(m, tt):
    d = m.shape[-1]
    if m.shape[1] == 1:
        return pl.BlockSpec((1, 1, d), lambda b, i, *_: (b, 0, 0))
    return pl.BlockSpec((1, tt, d), lambda b, i, *_: (b, i, 0))


def _inproj(x, sc, sh, g, w_rows, w_lr, lr0, tt):
    bsz, t, d = x.shape
    n = w_rows.shape[0] - GLA_GATE_RANK
    tc = 1024
    assert lr0 % tc == 0 and n == 2 * lr0
    wrow = lambda b, i, j: (
        pl.multiple_of(j * tc + (j // (lr0 // tc)) * GLA_GATE_RANK, GLA_GATE_RANK), 0)
    return pl.pallas_call(
        _inproj_kernel,
        grid=(bsz, t // tt, n // tc),
        in_specs=[pl.BlockSpec((1, tt, d), lambda b, i, j: (b, i, 0)),
                  _mod_spec(sc, tt), _mod_spec(sh, tt),
                  pl.BlockSpec((1, d), lambda b, i, j: (0, 0)),
                  pl.BlockSpec((pl.Element(tc), pl.Element(d)), wrow),
                  pl.BlockSpec((LANES, d), lambda b, i, j: (0, 0))],
        out_specs=[pl.BlockSpec((1, tt, tc), lambda b, i, j: (b, i, j)),
                   pl.BlockSpec((1, tt, LANES), lambda b, i, j: (b, i, 0))],
        out_shape=[jax.ShapeDtypeStruct((bsz, t, n), F32),
                   jax.ShapeDtypeStruct((bsz, t, LANES), F32)],
        scratch_shapes=[pltpu.VMEM((tt, d), BF16)],
        compiler_params=_params("parallel", "parallel", "arbitrary"),
        name="inproj",
    )(x, sc, sh, g, w_rows, w_lr)


def _gla_kernel(q_ref, k_ref, v_ref, r_ref, lr_ref, wau_ref, ba_ref, gg_ref, s0_ref,
                o_ref, s_ref, st_scr, *, dk, dv):
    c = pl.program_id(1)
    streams, rows = q_ref.shape[0], q_ref.shape[1]

    @pl.when(c == 0)
    def _():
        for s in range(streams):
            for h in range(GLA_HEADS):
                st_scr[s, h] = s0_ref[s, h].T

    for s in range(streams):
        _gla_chunk(s, q_ref, k_ref, v_ref, r_ref, lr_ref, wau_ref, ba_ref, gg_ref, o_ref, st_scr,
                   dk=dk, dv=dv)

    @pl.when(c == pl.num_programs(1) - 1)
    def _():
        for s in range(streams):
            for h in range(GLA_HEADS):
                s_ref[s, h] = st_scr[s, h].T


def _gla_chunk(s, q_ref, k_ref, v_ref, r_ref, lr_ref, wau_ref, ba_ref, gg_ref, o_ref, st_scr,
               *, dk, dv):
    rows = q_ref.shape[1]
    pre = jnp.dot(lr_ref[s].astype(BF16), wau_ref[...], preferred_element_type=F32) + ba_ref[...]
    log_a = (jnp.minimum(pre, 0.0) - jnp.log1p(jnp.exp(-jnp.abs(pre)))) * (1.0 / GLA_GATE_NORM)
    row = lax.broadcasted_iota(jnp.int32, log_a.shape, 0)
    b = log_a
    shift = 1
    while shift < rows:
        b = b + jnp.where(row >= shift, pltpu.roll(b, shift, axis=0), 0.0)
        shift *= 2
    b_last = b[rows - 1:rows, :]
    e_pos = jnp.exp(b)
    q_in = q_ref[s] * (dk ** -0.5) * e_pos
    kk = k_ref[s]
    k_in = kk * jnp.exp(-b)
    k_out = kk * jnp.exp(b_last - b)
    e_last = jnp.exp(b_last)
    causal = (lax.broadcasted_iota(jnp.int32, (rows, rows), 0)
              >= lax.broadcasted_iota(jnp.int32, (rows, rows), 1))
    gg = gg_ref[...]
    for h in range(GLA_HEADS):
        ks = slice(h * dk, (h + 1) * dk)
        vs = slice(h * dv, (h + 1) * dv)
        qh = q_in[:, ks].astype(BF16)
        vh = v_ref[s, :, vs].astype(BF16)
        st = st_scr[s, h]
        att = lax.dot_general(qh, k_in[:, ks].astype(BF16), (((1,), (1,)), ((), ())),
                              preferred_element_type=F32)
        att = jnp.where(causal, att, 0.0).astype(BF16)
        o = jnp.dot(att, vh, preferred_element_type=F32)
        o = o + lax.dot_general(qh, st.astype(BF16), (((1,), (1,)), ((), ())),
                                preferred_element_type=F32)
        st_scr[s, h] = st * e_last[:, ks] + lax.dot_general(
            vh, k_out[:, ks].astype(BF16), (((0,), (0,)), ((), ())), preferred_element_type=F32)
        o = _rms(o) * gg * _silu(r_ref[s, :, vs])
        o_ref[s, :, vs] = o.astype(o_ref.dtype)


def _gla(z, zlr, w_au, b_alpha, g_gla_out, s0):
    bsz, t, _ = z.shape
    _, nh, dk, dv = s0.shape
    rows = min(CHUNK, t)
    sb = 2 if bsz % 2 == 0 else 1
    kern = functools.partial(_gla_kernel, dk=dk, dv=dv)
    return pl.pallas_call(
        kern,
        grid=(bsz // sb, t // rows),
        in_specs=[pl.BlockSpec((sb, rows, nh * dk), lambda b, c: (b, c, 0)),
                  pl.BlockSpec((sb, rows, nh * dk), lambda b, c: (b, c, 1)),
                  pl.BlockSpec((sb, rows, nh * dv), lambda b, c: (b, c, 1)),
                  pl.BlockSpec((sb, rows, nh * dv), lambda b, c: (b, c, 2)),
                  pl.BlockSpec((sb, rows, LANES), lambda b, c: (b, c, 0)),
                  pl.BlockSpec((LANES, nh * dk), lambda b, c: (0, 0)),
                  pl.BlockSpec((1, nh * dk), lambda b, c: (0, 0)),
                  pl.BlockSpec((1, dv), lambda b, c: (0, 0)),
                  pl.BlockSpec((sb, nh, dk, dv), lambda b, c: (b, 0, 0, 0))],
        out_specs=[pl.BlockSpec((sb, rows, nh * dv), lambda b, c: (b, c, 0)),
                   pl.BlockSpec((sb, nh, dk, dv), lambda b, c: (b, 0, 0, 0))],
        out_shape=[jax.ShapeDtypeStruct((bsz, t, nh * dv), BF16),
                   jax.ShapeDtypeStruct((bsz, nh, dk, dv), F32)],
        scratch_shapes=[pltpu.VMEM((sb, nh, dv, dk), F32)],
        compiler_params=_params("parallel", "arbitrary"),
        name="gla",
    )(z, z, z, z, zlr, w_au, b_alpha, g_gla_out, s0)


def _conv_kernel(a_ref, g_ref, cache_ref, wdw_ref, bdw_ref, gln_ref, bln_ref,
                 y_ref, newc_ref, shift_scr, conv_scr):
    i = pl.program_id(1)
    tt = a_ref.shape[1]
    hist = CONV_WIDTH - 1
    pad = CONV_HALO - hist
    win_scr = shift_scr.at[0]

    @pl.when(i == 0)
    def _():
        win_scr[0:pad, :] = jnp.zeros((pad, win_scr.shape[1]), F32)
        win_scr[pad:CONV_HALO, :] = cache_ref[0]

    win_scr[CONV_HALO:CONV_HALO + tt, :] = a_ref[0] * _sigmoid(g_ref[0])
    full = win_scr[...]
    for s in range(1, SUBLANES):
        shift_scr[s] = pltpu.roll(full, full.shape[0] - s, axis=0)
    bdw = bdw_ref[...]
    gln = gln_ref[...]
    bln = bln_ref[...]

    rows = 2 * SUBLANES

    def chunk(ci, carry):
        r0 = pl.multiple_of(ci * rows, rows)
        acc = jnp.zeros((2, SUBLANES, win_scr.shape[1]), F32)
        for w in range(CONV_WIDTH):
            q, s = divmod(pad + w, SUBLANES)
            x = shift_scr[s, pl.ds(r0 + q * SUBLANES, rows), :]
            acc = acc + x.reshape(2, SUBLANES, x.shape[1]) * wdw_ref[w]
        conv_scr[pl.ds(r0, rows), :] = acc.reshape(rows, win_scr.shape[1])
        return carry

    lax.fori_loop(0, tt // rows, chunk, 0)
    acc = conv_scr[...] + bdw
    mu = jnp.mean(acc, axis=-1, keepdims=True)
    xc = acc - mu
    yn = xc * lax.rsqrt(jnp.mean(xc * xc, axis=-1, keepdims=True) + EPS) * gln + bln
    y_ref[0] = _silu(yn).astype(y_ref.dtype)

    @pl.when(i == pl.num_programs(1) - 1)
    def _():
        newc_ref[0] = win_scr[tt + pad:tt + CONV_HALO, :]

    tail = win_scr[tt:tt + CONV_HALO, :]
    win_scr[0:CONV_HALO, :] = tail


def _conv(z, cache, w_dw, b_dw, g_ln, b_ln, tt):
    bsz, t, _ = z.shape
    _, hist, ch = cache.shape
    return pl.pallas_call(
        _conv_kernel,
        grid=(bsz, t // tt),
        in_specs=[pl.BlockSpec((1, tt, ch), lambda b, i: (b, i, 6)),
                  pl.BlockSpec((1, tt, ch), lambda b, i: (b, i, 7)),
                  pl.BlockSpec((1, hist, ch), lambda b, i: (b, 0, 0)),
                  pl.BlockSpec((CONV_WIDTH, SUBLANES, ch), lambda b, i: (0, 0, 0)),
                  pl.BlockSpec((1, ch), lambda b, i: (0, 0)),
                  pl.BlockSpec((1, ch), lambda b, i: (0, 0)),
                  pl.BlockSpec((1, ch), lambda b, i: (0, 0))],
        out_specs=[pl.BlockSpec((1, tt, ch), lambda b, i: (b, i, 0)),
                   pl.BlockSpec((1, hist, ch), lambda b, i: (b, 0, 0))],
        out_shape=[jax.ShapeDtypeStruct((bsz, t, ch), BF16),
                   jax.ShapeDtypeStruct((bsz, hist, ch), F32)],
        scratch_shapes=[pltpu.VMEM((SUBLANES, tt + CONV_HALO, ch), F32), pltpu.VMEM((tt, ch), F32)],
        compiler_params=_params("parallel", "arbitrary"),
        name="conv",
    )(z, z, cache, w_dw, b_dw, g_ln, b_ln)


def _merge_kernel(oa_ref, cb_ref, ga_ref, gb_ref, x_ref, gt1_ref, sc2_ref, sh2_ref,
                  gpost_ref, gpre_ref, wgo_ref, wco_ref, wout_ref, x1_ref, h2_ref):
    ya = jnp.dot(oa_ref[0], wgo_ref[...], preferred_element_type=F32)
    yb = jnp.dot(cb_ref[0], wco_ref[...], preferred_element_type=F32)
    m = _sigmoid(ga_ref[0]) * ya + _sigmoid(gb_ref[0]) * yb
    mix = jnp.dot(m.astype(BF16), wout_ref[...], preferred_element_type=F32)
    x1 = x_ref[0] + gt1_ref[0] * (_rms(mix) * gpost_ref[...])
    x1_ref[0] = x1
    h2 = _rms(x1) * gpre_ref[...]
    h2_ref[0] = (h2 * (1.0 + sc2_ref[0]) + sh2_ref[0]).astype(h2_ref.dtype)


def _merge(oa, cb, z, x, gt1, sc2, sh2, g_post, g_pre, w_go, w_co, w_out, tt):
    bsz, t, d = x.shape
    ch = cb.shape[-1]
    tok = lambda w: pl.BlockSpec((1, tt, w), lambda b, i: (b, i, 0))
    return pl.pallas_call(
        _merge_kernel,
        grid=(bsz, t // tt),
        in_specs=[tok(d), tok(ch),
                  pl.BlockSpec((1, tt, d), lambda b, i: (b, i, 4)),
                  pl.BlockSpec((1, tt, d), lambda b, i: (b, i, 5)),
                  tok(d), _mod_spec(gt1, tt), _mod_spec(sc2, tt), _mod_spec(sh2, tt),
                  _const_spec((1, d)), _const_spec((1, d)),
                  _const_spec(w_go.shape), _const_spec(w_co.shape), _const_spec(w_out.shape)],
        out_specs=[tok(d), tok(d)],
        out_shape=[jax.ShapeDtypeStruct((bsz, t, d), F32),
                   jax.ShapeDtypeStruct((bsz, t, d), BF16)],
        compiler_params=_params("parallel", "parallel"),
        name="merge",
    )(oa, cb, z, z, x, gt1, sc2, sh2, g_post, g_pre, w_go, w_co, w_out)


def _keyfold_kernel(ks_ref, wq_ref, o_ref):
    o_ref[...] = lax.dot_general(ks_ref[0], wq_ref[...], (((1,), (1,)), ((), ())),
                                 preferred_element_type=F32,
                                 precision=lax.Precision.HIGHEST).astype(o_ref.dtype)


def _keyfold(k_sub, w_pq):
    d = w_pq.shape[0]
    _, nk, dq = k_sub.shape
    nblk = w_pq.shape[1] // dq
    return pl.pallas_call(
        _keyfold_kernel,
        grid=(nblk,),
        in_specs=[pl.BlockSpec((1, nk, dq), lambda j: (j % 2, 0, 0)),
                  pl.BlockSpec((d, dq), lambda j: (0, j))],
        out_specs=pl.BlockSpec((nk, d), lambda j: (j, 0)),
        out_shape=jax.ShapeDtypeStruct((nblk * nk, d), BF16),
        compiler_params=_params("arbitrary"),
        name="keyfold",
    )(k_sub, w_pq)


def _top16(s, vals, idxs):
    rows = s.shape[0]
    rid = lax.broadcasted_iota(jnp.int32, s.shape, 0).astype(F32)
    for _ in range(PEER_TOPK):
        m = jnp.max(s, axis=0, keepdims=True)
        am = jnp.min(jnp.where(s == m, rid, float(rows)), axis=0, keepdims=True)
        vals.append(m)
        idxs.append(am)
        s = jnp.where(rid == am, -jnp.inf, s)
        yield m


def _stack_rows(rows_list, n):
    k = len(rows_list)
    rid = lax.broadcasted_iota(jnp.int32, (k, n), 0)
    out = jnp.zeros((k, n), F32)
    for r, v in enumerate(rows_list):
        out = jnp.where(rid == r, v, out)
    return out


def _topk_columns(s1, s2, result):
    tn = s1.shape[1]
    nk = PEER_NKEYS
    v1, i1, v2, i2 = [], [], [], []
    for m1, m2 in zip(_top16(s1, v1, i1), _top16(s2, v2, i2)):
        yield m1 + m2
    v2a = _stack_rows(v2, tn)
    i2a = _stack_rows(i2, tn)
    v1a = _stack_rows(v1, tn)
    i1a = _stack_rows(i1, tn)
    sub = lax.broadcasted_iota(jnp.int32, (SUBLANES, tn), 0)
    cand, cidx = [], []
    for a in range(SUBLANES):
        lim = PEER_TOPK // (a + 1)
        for b0 in range(0, lim, SUBLANES):
            cv = v1[a] + v2a[b0:b0 + SUBLANES]
            ci = i1[a] * float(nk) + i2a[b0:b0 + SUBLANES]
            if lim - b0 < SUBLANES:
                cv = jnp.where(sub < lim - b0, cv, -jnp.inf)
            cand.append(cv)
            cidx.append(ci)
    cand.append(v1a[SUBLANES:] + v2[0])
    cidx.append(i1a[SUBLANES:] * float(nk) + i2[0])
    cand = jnp.concatenate(cand, axis=0)
    cidx = jnp.concatenate(cidx, axis=0)
    rid = lax.broadcasted_iota(jnp.int32, cand.shape, 0).astype(F32)
    sc, ids = [], []
    for _ in range(PEER_TOPK):
        m = jnp.max(cand, axis=0, keepdims=True)
        am = jnp.min(jnp.where(cand == m, rid, float(cand.shape[0])), axis=0, keepdims=True)
        hit = rid == am
        sc.append(m)
        ids.append(jnp.sum(jnp.where(hit, cidx, 0.0), axis=0, keepdims=True))
        cand = jnp.where(hit, -jnp.inf, cand)
        yield m
    sca = _stack_rows(sc, tn)
    ida = _stack_rows(ids, tn)
    e = jnp.exp(sca - sc[0])
    wts = e / jnp.sum(e, axis=0, keepdims=True)
    ia = jnp.floor(ida * (1.0 / nk))
    result.extend([wts, ia, ida - ia * float(nk)])


def _act_kernel(hn_ref, tab_ref, h_ref, u_ref, coef_ref, a_ref, b_ref,
                sel_scr, s_scr, sk_scr, wt_scr, at_scr, bt_scr, w_tok, a_tok, b_tok,
                *, width, units):
    t = pl.program_id(0)
    k = pl.program_id(1)
    last = pl.num_programs(1) - 1
    tn = h_ref.shape[0]
    nb = u_ref.shape[0] // LANES
    nk = PEER_NKEYS
    blocks = tn // width

    def scores(g):
        head = (g * units) // blocks
        r0 = pl.multiple_of(((g * units) % blocks) * width, width)
        tab = tab_ref[pl.ds(pl.multiple_of(head * (2 * nk), 2 * nk), 2 * nk), :]
        return lax.dot_general(tab, hn_ref[pl.ds(r0, units * width), :],
                               (((1,), (1,)), ((), ())), preferred_element_type=F32)

    def topk_units(side_work=()):
        head = (k * units) // blocks
        r0 = pl.multiple_of(((k * units) % blocks) * width, width)
        row0 = pl.multiple_of(head * PEER_TOPK, PEER_TOPK)
        results = [[] for _ in range(units)]
        gens = [_topk_columns(sk_scr[0:nk, j * width:(j + 1) * width],
                              sk_scr[nk:2 * nk, j * width:(j + 1) * width], results[j])
                for j in range(units)]
        side_work = list(side_work)
        every = -(-2 * PEER_TOPK // max(len(side_work), 1))
        point, progress = 0, []
        while gens:
            if side_work and point % every == 0:
                side_work.pop(0)(progress)
            alive, progress = [], []
            for g in gens:
                value = next(g, None)
                if value is not None:
                    alive.append(g)
                    progress.append(value)
            gens = alive
            point += 1
        assert not side_work
        for j, (wts, ia, ib) in enumerate(results):
            cols = pl.ds(pl.multiple_of(r0 + j * width, width), width)
            wt_scr[pl.ds(row0, PEER_TOPK), cols] = wts
            at_scr[pl.ds(row0, PEER_TOPK), cols] = ia
            bt_scr[pl.ds(row0, PEER_TOPK), cols] = ib
        sk_scr[...] = scores(jnp.minimum(k + 1, last))

    def gather(key0, sel):
        ia = a_tok[...]
        ib = b_tok[...].astype(jnp.int32)
        for j in range(nb):
            g = jnp.take_along_axis(s_scr[:, j * LANES:(j + 1) * LANES], ib, axis=1)
            sel = jnp.where(ia == key0 + float(j), g, sel)
        return sel

    base = (k * nb).astype(F32)

    @pl.when(k == 0)
    def _():
        sel_scr[...] = jnp.zeros(sel_scr.shape, F32)
        s_scr[...] = jnp.zeros(s_scr.shape, F32)
        sk_scr[...] = scores(0)

    @pl.when(t == 0)
    def _():
        topk_units()

    @pl.when(t > 0)
    def _():
        sel_scr[...] = gather(base - float(nb), sel_scr[...])

        def slab(e0):
            def run(progress):
                ub = u_ref[e0:e0 + MXU_WIDTH, :]
                for value in progress:
                    ub = ub + value[:, 0:1] * 0.0
                s_scr[:, e0:e0 + MXU_WIDTH] = lax.dot_general(
                    h_ref[...], ub.astype(BF16), (((1,), (1,)), ((), ())),
                    preferred_element_type=F32)
            return run

        topk_units([slab(e0) for e0 in range(0, u_ref.shape[0], MXU_WIDTH)])

    @pl.when((k == last) & (t > 0))
    def _():
        x = gather(base, sel_scr[...])
        act = 0.5 * x * (1.0 + lax.erf(x * (2.0 ** -0.5)))
        coef_ref[...] = w_tok[...] * act

    @pl.when(k == last)
    def _():
        w_tok[...] = wt_scr[...].T
        a_tok[...] = at_scr[...].T
        b_tok[...] = bt_scr[...].T
        a_ref[...] = a_tok[...]
        b_ref[...] = b_tok[...]


def _expert_act(h2, table, u_exp, tn, te):
    n, d = h2.shape
    ne = u_exp.shape[0]
    nsel = PEER_HEADS * PEER_TOPK
    tiles, steps = n // tn, ne // te
    width = min(tn, 2 * LANES)
    blocks = tn // width
    units = PEER_HEADS * blocks // steps
    assert units * steps == PEER_HEADS * blocks and blocks % units == 0
    out = jax.ShapeDtypeStruct((n, nsel), F32)
    tok = lambda shape: pltpu.VMEM(shape, F32)
    return pl.pallas_call(
        functools.partial(_act_kernel, width=width, units=units),
        grid=(tiles + 1, steps),
        in_specs=[pl.BlockSpec((tn, d), lambda t, k: (jnp.minimum(t, tiles - 1), 0),
                               pipeline_mode=pl.Buffered(1)),
                  _const_spec(table.shape),
                  pl.BlockSpec((tn, d), lambda t, k: (jnp.maximum(t - 1, 0), 0),
                               pipeline_mode=pl.Buffered(1)),
                  pl.BlockSpec((te, d), lambda t, k: (jnp.where(t == 0, 0, k), 0))],
        out_specs=[pl.BlockSpec((tn, nsel), lambda t, k: (jnp.maximum(t - 1, 0), 0)),
                   pl.BlockSpec((tn, nsel), lambda t, k: (jnp.minimum(t, tiles - 1), 0)),
                   pl.BlockSpec((tn, nsel), lambda t, k: (jnp.minimum(t, tiles - 1), 0))],
        out_shape=[out, out, out],
        scratch_shapes=([tok((tn, nsel)), tok((tn, te)), tok((2 * PEER_NKEYS, units * width))]
                        + [tok((nsel, tn))] * 3 + [tok((tn, nsel))] * 3),
        compiler_params=_params("arbitrary", "arbitrary"),
        name="expert_act",
    )(h2, table, h2, u_exp)


def _mixmat_kernel(*refs, cast_values):
    if cast_values:
        coef_ref, a_ref, b_ref, v_ref, p_ref, vb_ref, p_scr = refs
        vb_ref[...] = v_ref[...].astype(vb_ref.dtype)
    else:
        coef_ref, a_ref, b_ref, p_ref, p_scr = refs
    tn = coef_ref.shape[0]
    nk = PEER_NKEYS
    kid = lax.broadcasted_iota(jnp.int32, (nk, coef_ref.shape[1]), 0).astype(F32).astype(BF16)
    one = jnp.ones((nk, coef_ref.shape[1]), BF16)
    zero = jnp.zeros((nk, coef_ref.shape[1]), BF16)

    unroll = 4 * SUBLANES

    def group(g, carry):
        t0 = pl.multiple_of(g * unroll, unroll)
        for s in range(unroll):
            c = coef_ref[pl.ds(t0 + s, 1), :].astype(BF16)
            ia = a_ref[pl.ds(t0 + s, 1), :].astype(BF16)
            ib = b_ref[pl.ds(t0 + s, 1), :].astype(BF16)
            lhs = jnp.where(ia == kid, c, zero)
            rhs = jnp.where(ib == kid, one, zero)
            row0 = (g * (unroll // SUBLANES) + s // SUBLANES) * (nk * SUBLANES) + s % SUBLANES
            p_scr[pl.ds(row0, nk, stride=SUBLANES), :] = lax.dot_general(
                lhs, rhs, (((1,), (1,)), ((), ())), preferred_element_type=F32)
        return carry

    lax.fori_loop(0, tn // unroll, group, 0)

    def emit(rr, carry):
        lo = pl.multiple_of(rr * (2 * nk * SUBLANES), SUBLANES)
        hi = lo + nk * SUBLANES
        row = pl.multiple_of(rr * 2 * SUBLANES, 2 * SUBLANES)
        for j in range(nk):
            tile = jnp.concatenate([p_scr[pl.ds(lo + j * SUBLANES, SUBLANES), :],
                                    p_scr[pl.ds(hi + j * SUBLANES, SUBLANES), :]], axis=0)
            p_ref[pl.ds(row, 2 * SUBLANES), j * nk:(j + 1) * nk] = tile.astype(p_ref.dtype)
        return carry

    lax.fori_loop(0, tn // (2 * SUBLANES), emit, 0)


def _mixmat(coef, ia, ib, tn, v_exp=None):
    n, nsel = coef.shape
    ne = PEER_NKEYS * PEER_NKEYS
    steps = n // tn
    sspec = pl.BlockSpec((tn, nsel), lambda i: (i, 0))
    in_specs, out_specs = [sspec, sspec, sspec], [pl.BlockSpec((tn, ne), lambda i: (i, 0))]
    out_shape, args = [jax.ShapeDtypeStruct((n, ne), BF16)], [coef, ia, ib]
    if v_exp is not None:
        d = v_exp.shape[1]
        vspec = pl.BlockSpec((ne // steps, d), lambda i: (i, 0))
        in_specs.append(vspec)
        out_specs.append(vspec)
        out_shape.append(jax.ShapeDtypeStruct((ne, d), BF16))
        args.append(v_exp)
    out = pl.pallas_call(
        functools.partial(_mixmat_kernel, cast_values=v_exp is not None),
        grid=(steps,),
        in_specs=in_specs,
        out_specs=out_specs,
        out_shape=out_shape,
        scratch_shapes=[pltpu.VMEM((tn * PEER_NKEYS, PEER_NKEYS), F32)],
        compiler_params=_params("parallel"),
        name="mixmat",
    )(*args)
    return out if v_exp is not None else (out[0], None)


def _value_kernel(p_ref, v_ref, x1_ref, gt2_ref, g_ref, y_ref, acc_scr):
    k = pl.program_id(2)

    @pl.when(k == 0)
    def _():
        acc_scr[...] = jnp.zeros(acc_scr.shape, F32)

    acc_scr[...] += jnp.dot(p_ref[0], v_ref[...], preferred_element_type=F32)

    @pl.when(k == pl.num_programs(2) - 1)
    def _():
        y_ref[0] = x1_ref[0] + gt2_ref[0] * (_rms(acc_scr[...]) * g_ref[...])


def _value(p, v_bf, x1, gt2, g_post, tt, tk):
    bsz, t, d = x1.shape
    ne = v_bf.shape[0]
    return pl.pallas_call(
        _value_kernel,
        grid=(bsz, t // tt, ne // tk),
        in_specs=[pl.BlockSpec((1, tt, tk), lambda b, i, k: (b, i, k)),
                  pl.BlockSpec((tk, d), lambda b, i, k: (k, 0)),
                  pl.BlockSpec((1, tt, d), lambda b, i, k: (b, i, 0),
                               pipeline_mode=pl.Buffered(1)),
                  _mod_spec(gt2, tt),
                  pl.BlockSpec((1, d), lambda b, i, k: (0, 0))],
        out_specs=pl.BlockSpec((1, tt, d), lambda b, i, k: (b, i, 0),
                               pipeline_mode=pl.Buffered(1)),
        out_shape=jax.ShapeDtypeStruct((bsz, t, d), F32),
        scratch_shapes=[pltpu.VMEM((tt, d), F32)],
        compiler_params=_params("parallel", "parallel", "arbitrary"),
        name="value",
    )(p, v_bf, x1, gt2, g_post)


def _layer(x, mods, s0, cache, wts, *, rec_shape, tt, v_bf=None):
    sh1, sc1, gt1, sh2, sc2, gt2 = mods
    bsz, t, d = x.shape
    z, zlr = _inproj(x, sc1, sh1, wts["g_pre_mix"], wts["w_rows"], wts["w_lr"], wts["lr0"], tt)
    rs, rt = rec_shape
    zr = z.reshape(rs, rt, z.shape[-1])
    oa, s_new = _gla(zr, zlr.reshape(rs, rt, LANES), wts["w_au"], wts["b_alpha"],
                     wts["g_gla_out"], s0)
    cb, c_new = _conv(zr, cache, wts["w_dw"], wts["b_dw"], wts["g_conv_ln"], wts["b_conv_ln"],
                      min(rt, 256))
    x1, h2 = _merge(oa.reshape(bsz, t, -1), cb.reshape(bsz, t, -1), z, x, gt1, sc2, sh2,
                    wts["g_post_mix"], wts["g_pre_ffn"], wts["w_gla_o"], wts["w_conv_o"],
                    wts["w_out"], min(t, 256))
    n = bsz * t
    h2f = h2.reshape(n, d)
    coef, ia, ib = _expert_act(h2f, wts["key_table"], wts["u_exp"], min(n, 1024),
                               1024 if n >= 1024 else 2048)
    p, v_new = _mixmat(coef, ia, ib, 128, None if v_bf is not None else wts["v_exp"])
    v_bf = v_bf if v_bf is not None else v_new
    y = _value(p.reshape(bsz, t, -1), v_bf, x1, gt2, wts["g_post_ffn"], min(t, 512), 2048)
    return y, s_new, c_new, v_bf


def kernel(x_prompt, x_sample, state_gla, cache_conv, c_prompt, c_sample, w_ada, b_ada, g_pre_mix,
           g_post_mix, w_in, w_alpha_up, b_alpha, g_gla_out, w_gla_o, w_dw, b_dw, g_conv_ln,
           b_conv_ln, w_conv_o, w_out, g_pre_ffn, g_post_ffn, w_pq, k_sub1, k_sub2, u_exp, v_exp):
    depth = w_ada.shape[0]
    bp, tp, d = x_prompt.shape
    bs, ts, _ = x_sample.shape
    nh, dk, dv = state_gla.shape[2:]
    xp, xs = x_prompt, x_sample.reshape(1, bs * ts, d)
    outs = [[], [], [], []]
    for l in range(depth):
        lr0 = 2 * nh * dk + 2 * nh * dv
        wl = w_in[l].T
        row = lambda a: a[l].reshape(1, -1)
        wts = dict(
            w_rows=wl, lr0=lr0,
            w_lr=jnp.pad(wl[lr0:lr0 + GLA_GATE_RANK],
                         ((0, LANES - GLA_GATE_RANK), (0, 0))).astype(BF16),
            w_au=jnp.pad(w_alpha_up[l], ((0, LANES - GLA_GATE_RANK), (0, 0))).astype(BF16),
            b_alpha=row(b_alpha), g_gla_out=row(g_gla_out), g_pre_mix=row(g_pre_mix),
            g_post_mix=row(g_post_mix), g_pre_ffn=row(g_pre_ffn), g_post_ffn=row(g_post_ffn),
            w_dw=jnp.broadcast_to(w_dw[l][:, None, :], (CONV_WIDTH, SUBLANES, w_dw.shape[-1])),
            b_dw=row(b_dw), g_conv_ln=row(g_conv_ln), b_conv_ln=row(b_conv_ln),
            w_gla_o=w_gla_o[l].astype(BF16), w_conv_o=w_conv_o[l].astype(BF16),
            w_out=w_out[l].astype(BF16),
            key_table=_keyfold(jnp.stack([k_sub1[l], k_sub2[l]]), w_pq[l]),
            u_exp=u_exp[l], v_exp=v_exp[l],
        )
        c_all = jnp.concatenate(
            [c_prompt, c_sample, jnp.zeros((-(bp + bs) % SUBLANES, d), F32)], axis=0)
        mod = _ada(c_all, w_ada[l], b_ada[l])
        mods_p = [m.reshape(bp, 1, d) for m in jnp.split(mod[:bp], 6, axis=-1)]
        mods_s = [jnp.repeat(m, ts, axis=0).reshape(1, bs * ts, d)
                  for m in jnp.split(mod[bp:bp + bs], 6, axis=-1)]
        s0_p = jnp.zeros((bp, nh, dk, dv), F32)
        cache0_p = jnp.zeros((bp, CONV_WIDTH - 1, cache_conv.shape[-1]), F32)
        xp, s_p, cb_p, v_bf = _layer(xp, mods_p, s0_p, cache0_p, wts, rec_shape=(bp, tp), tt=1024)
        xs, s_s, cb_s, _ = _layer(xs, mods_s, state_gla[l], cache_conv[l], wts,
                                  rec_shape=(bs, ts), tt=bs * ts, v_bf=v_bf)
        for o, v in zip(outs, (s_p, cb_p, s_s, cb_s)):
            o.append(v)
    return (xp, xs.reshape(bs, ts, d), jnp.stack(outs[0]), jnp.stack(outs[1]),
            jnp.stack(outs[2]), jnp.stack(outs[3]))
```

```python
import functools

import jax
import jax.numpy as jnp
from jax import lax
from jax.experimental import pallas as pl
from jax.experimental.pallas import tpu as pltpu

F32 = jnp.float32
BF16 = jnp.bfloat16

CHUNK = 64
GLA_HEADS = 4
GLA_GATE_RANK = 16
GLA_GATE_NORM = 16.0
CONV_WIDTH = 31
PEER_HEADS = 8
PEER_NKEYS = 128
PEER_TOPK = 16
EPS = 1e-6

LANES = 128
SUBLANES = 8
MXU_WIDTH = 256
CONV_HALO = 32
VMEM_LIMIT = 52 * 1024 * 1024


def _params(*sem):
    return pltpu.CompilerParams(dimension_semantics=sem, vmem_limit_bytes=VMEM_LIMIT)


def _const_spec(shape):
    nd = len(shape)
    return pl.BlockSpec(shape, lambda *_: (0,) * nd, pipeline_mode=pl.Buffered(1))


def _rms(x):
    return x * lax.rsqrt(jnp.mean(x * x, axis=-1, keepdims=True) + EPS)


def _sigmoid(x):
    return 1.0 / (1.0 + jnp.exp(-x))


def _silu(x):
    return x * _sigmoid(x)


def _ada_kernel(c_ref, w_ref, b_ref, o_ref):
    c = c_ref[...]
    o_ref[...] = jnp.dot(_silu(c).astype(BF16), w_ref[...].astype(BF16),
                         preferred_element_type=F32) + b_ref[...]


def _ada(c, w_ada, b_ada):
    g, d = c.shape
    n = w_ada.shape[1]
    tc = 1024
    return pl.pallas_call(
        _ada_kernel,
        grid=(n // tc,),
        in_specs=[pl.BlockSpec((g, d), lambda j: (0, 0)),
                  pl.BlockSpec((d, tc), lambda j: (0, j)),
                  pl.BlockSpec((1, tc), lambda j: (0, j))],
        out_specs=pl.BlockSpec((g, tc), lambda j: (0, j)),
        out_shape=jax.ShapeDtypeStruct((g, n), F32),
        compiler_params=_params("arbitrary"),
        name="ada",
    )(c, w_ada, b_ada.reshape(1, n))


def _inproj_kernel(x_ref, sc_ref, sh_ref, g_ref, w_ref, wlr_ref, z_ref, zlr_ref, h_scr):
    @pl.when(pl.program_id(2) == 0)
    def _():
        h = _rms(x_ref[0]) * g_ref[...]
        h = (h * (1.0 + sc_ref[0]) + sh_ref[0]).astype(BF16)
        h_scr[...] = h
        zlr_ref[0] = lax.dot_general(h, wlr_ref[...], (((1,), (1,)), ((), ())),
                                     preferred_element_type=F32)

    z_ref[0] = lax.dot_general(h_scr[...], w_ref[...].astype(BF16), (((1,), (1,)), ((), ())),
                               preferred_element_type=F32)


def _mod_spec(m, tt):
    d = m.shape[-1]
    if m.shape[1] == 1:
        return pl.BlockSpec((1, 1, d), lambda b, i, *_: (b, 0, 0))
    return pl.BlockSpec((1, tt, d), lambda b, i, *_: (b, i, 0))


def _inproj(x, sc, sh, g, w_rows, w_lr, lr0, tt):
    bsz, t, d = x.shape
    n = w_rows.shape[0] - GLA_GATE_RANK
    tc = 1024
    assert lr0 % tc == 0 and n == 2 * lr0
    wrow = lambda b, i, j: (
        pl.multiple_of(j * tc + (j // (lr0 // tc)) * GLA_GATE_RANK, GLA_GATE_RANK), 0)
    return pl.pallas_call(
        _inproj_kernel,
        grid=(bsz, t // tt, n // tc),
        in_specs=[pl.BlockSpec((1, tt, d), lambda b, i, j: (b, i, 0)),
                  _mod_spec(sc, tt), _mod_spec(sh, tt),
                  pl.BlockSpec((1, d), lambda b, i, j: (0, 0)),
                  pl.BlockSpec((pl.Element(tc), pl.Element(d)), wrow),
                  pl.BlockSpec((LANES, d), lambda b, i, j: (0, 0))],
        out_specs=[pl.BlockSpec((1, tt, tc), lambda b, i, j: (b, i, j)),
                   pl.BlockSpec((1, tt, LANES), lambda b, i, j: (b, i, 0))],
        out_shape=[jax.ShapeDtypeStruct((bsz, t, n), F32),
                   jax.ShapeDtypeStruct((bsz, t, LANES), F32)],
        scratch_shapes=[pltpu.VMEM((tt, d), BF16)],
        compiler_params=_params("parallel", "parallel", "arbitrary"),
        name="inproj",
    )(x, sc, sh, g, w_rows, w_lr)


def _gla_kernel(q_ref, k_ref, v_ref, r_ref, lr_ref, wau_ref, ba_ref, gg_ref, s0_ref,
                o_ref, s_ref, st_scr, *, dk, dv):
    c = pl.program_id(1)
    streams, rows = q_ref.shape[0], q_ref.shape[1]

    @pl.when(c == 0)
    def _():
        for s in range(streams):
            for h in range(GLA_HEADS):
                st_scr[s, h] = s0_ref[s, h].T

    for s in range(streams):
        _gla_chunk(s, q_ref, k_ref, v_ref, r_ref, lr_ref, wau_ref, ba_ref, gg_ref, o_ref, st_scr,
                   dk=dk, dv=dv)

    @pl.when(c == pl.num_programs(1) - 1)
    def _():
        for s in range(streams):
            for h in range(GLA_HEADS):
                s_ref[s, h] = st_scr[s, h].T


def _gla_chunk(s, q_ref, k_ref, v_ref, r_ref, lr_ref, wau_ref, ba_ref, gg_ref, o_ref, st_scr,
               *, dk, dv):
    rows = q_ref.shape[1]
    pre = jnp.dot(lr_ref[s].astype(BF16), wau_ref[...], preferred_element_type=F32) + ba_ref[...]
    log_a = (jnp.minimum(pre, 0.0) - jnp.log1p(jnp.exp(-jnp.abs(pre)))) * (1.0 / GLA_GATE_NORM)
    row = lax.broadcasted_iota(jnp.int32, log_a.shape, 0)
    b = log_a
    shift = 1
    while shift < rows:
        b = b + jnp.where(row >= shift, pltpu.roll(b, shift, axis=0), 0.0)
        shift *= 2
    b_last = b[rows - 1:rows, :]
    e_pos = jnp.exp(b)
    q_in = q_ref[s] * (dk ** -0.5) * e_pos
    kk = k_ref[s]
    k_in = kk * jnp.exp(-b)
    k_out = kk * jnp.exp(b_last - b)
    e_last = jnp.exp(b_last)
    causal = (lax.broadcasted_iota(jnp.int32, (rows, rows), 0)
              >= lax.broadcasted_iota(jnp.int32, (rows, rows), 1))
    gg = gg_ref[...]
    for h in range(GLA_HEADS):
        ks = slice(h * dk, (h + 1) * dk)
        vs = slice(h * dv, (h + 1) * dv)
        qh = q_in[:, ks].astype(BF16)
        vh = v_ref[s, :, vs].astype(BF16)
        st = st_scr[s, h]
        att = lax.dot_general(qh, k_in[:, ks].astype(BF16), (((1,), (1,)), ((), ())),
                              preferred_element_type=F32)
        att = jnp.where(causal, att, 0.0).astype(BF16)
        o = jnp.dot(att, vh, preferred_element_type=F32)
        o = o + lax.dot_general(qh, st.astype(BF16), (((1,), (1,)), ((), ())),
                                preferred_element_type=F32)
        st_scr[s, h] = st * e_last[:, ks] + lax.dot_general(
            vh, k_out[:, ks].astype(BF16), (((0,), (0,)), ((), ())), preferred_element_type=F32)
        o = _rms(o) * gg * _silu(r_ref[s, :, vs])
        o_ref[s, :, vs] = o.astype(o_ref.dtype)


def _gla(z, zlr, w_au, b_alpha, g_gla_out, s0):
    bsz, t, _ = z.shape
    _, nh, dk, dv = s0.shape
    rows = min(CHUNK, t)
    sb = 2 if bsz % 2 == 0 else 1
    kern = functools.partial(_gla_kernel, dk=dk, dv=dv)
    return pl.pallas_call(
        kern,
        grid=(bsz // sb, t // rows),
        in_specs=[pl.BlockSpec((sb, rows, nh * dk), lambda b, c: (b, c, 0)),
                  pl.BlockSpec((sb, rows, nh * dk), lambda b, c: (b, c, 1)),
                  pl.BlockSpec((sb, rows, nh * dv), lambda b, c: (b, c, 1)),
                  pl.BlockSpec((sb, rows, nh * dv), lambda b, c: (b, c, 2)),
                  pl.BlockSpec((sb, rows, LANES), lambda b, c: (b, c, 0)),
                  pl.BlockSpec((LANES, nh * dk), lambda b, c: (0, 0)),
                  pl.BlockSpec((1, nh * dk), lambda b, c: (0, 0)),
                  pl.BlockSpec((1, dv), lambda b, c: (0, 0)),
                  pl.BlockSpec((sb, nh, dk, dv), lambda b, c: (b, 0, 0, 0))],
        out_specs=[pl.BlockSpec((sb, rows, nh * dv), lambda b, c: (b, c, 0)),
                   pl.BlockSpec((sb, nh, dk, dv), lambda b, c: (b, 0, 0, 0))],
        out_shape=[jax.ShapeDtypeStruct((bsz, t, nh * dv), BF16),
                   jax.ShapeDtypeStruct((bsz, nh, dk, dv), F32)],
        scratch_shapes=[pltpu.VMEM((sb, nh, dv, dk), F32)],
        compiler_params=_params("parallel", "arbitrary"),
        name="gla",
    )(z, z, z, z, zlr, w_au, b_alpha, g_gla_out, s0)


def _conv_kernel(a_ref, g_ref, cache_ref, wdw_ref, bdw_ref, gln_ref, bln_ref,
                 y_ref, newc_ref, shift_scr, conv_scr):
    i = pl.program_id(1)
    tt = a_ref.shape[1]
    hist = CONV_WIDTH - 1
    pad = CONV_HALO - hist
    win_scr = shift_scr.at[0]

    @pl.when(i == 0)
    def _():
        win_scr[0:pad, :] = jnp.zeros((pad, win_scr.shape[1]), F32)
        win_scr[pad:CONV_HALO, :] = cache_ref[0]

    win_scr[CONV_HALO:CONV_HALO + tt, :] = a_ref[0] * _sigmoid(g_ref[0])
    full = win_scr[...]
    for s in range(1, SUBLANES):
        shift_scr[s] = pltpu.roll(full, full.shape[0] - s, axis=0)
    bdw = bdw_ref[...]
    gln = gln_ref[...]
    bln = bln_ref[...]

    rows = min(tt, 4 * SUBLANES)

    def chunk(ci, carry):
        r0 = pl.multiple_of(ci * rows, rows)
        acc = jnp.zeros((rows // SUBLANES, SUBLANES, win_scr.shape[1]), F32)
        for w in range(CONV_WIDTH):
            q, s = divmod(pad + w, SUBLANES)
            x = shift_scr[s, pl.ds(r0 + q * SUBLANES, rows), :]
            acc = acc + x.reshape(rows // SUBLANES, SUBLANES, x.shape[1]) * wdw_ref[w]
        conv_scr[pl.ds(r0, rows), :] = acc.reshape(rows, win_scr.shape[1])
        return carry

    lax.fori_loop(0, tt // rows, chunk, 0)
    acc = conv_scr[...] + bdw
    mu = jnp.mean(acc, axis=-1, keepdims=True)
    xc = acc - mu
    yn = xc * lax.rsqrt(jnp.mean(xc * xc, axis=-1, keepdims=True) + EPS) * gln + bln
    y_ref[0] = _silu(yn).astype(y_ref.dtype)

    @pl.when(i == pl.num_programs(1) - 1)
    def _():
        newc_ref[0] = win_scr[tt + pad:tt + CONV_HALO, :]

    tail = win_scr[tt:tt + CONV_HALO, :]
    win_scr[0:CONV_HALO, :] = tail


def _conv(z, cache, w_dw, b_dw, g_ln, b_ln, tt):
    bsz, t, _ = z.shape
    _, hist, ch = cache.shape
    return pl.pallas_call(
        _conv_kernel,
        grid=(bsz, t // tt),
        in_specs=[pl.BlockSpec((1, tt, ch), lambda b, i: (b, i, 6)),
                  pl.BlockSpec((1, tt, ch), lambda b, i: (b, i, 7)),
                  pl.BlockSpec((1, hist, ch), lambda b, i: (b, 0, 0)),
                  pl.BlockSpec((CONV_WIDTH, SUBLANES, ch), lambda b, i: (0, 0, 0)),
                  pl.BlockSpec((1, ch), lambda b, i: (0, 0)),
                  pl.BlockSpec((1, ch), lambda b, i: (0, 0)),
                  pl.BlockSpec((1, ch), lambda b, i: (0, 0))],
        out_specs=[pl.BlockSpec((1, tt, ch), lambda b, i: (b, i, 0)),
                   pl.BlockSpec((1, hist, ch), lambda b, i: (b, 0, 0))],
        out_shape=[jax.ShapeDtypeStruct((bsz, t, ch), BF16),
                   jax.ShapeDtypeStruct((bsz, hist, ch), F32)],
        scratch_shapes=[pltpu.VMEM((SUBLANES, tt + CONV_HALO, ch), F32), pltpu.VMEM((tt, ch), F32)],
        compiler_params=_params("parallel", "arbitrary"),
        name="conv",
    )(z, z, cache, w_dw, b_dw, g_ln, b_ln)


def _merge_kernel(oa_ref, cb_ref, ga_ref, gb_ref, x_ref, gt1_ref, sc2_ref, sh2_ref,
                  gpost_ref, gpre_ref, wgo_ref, wco_ref, wout_ref, x1_ref, h2_ref):
    ya = jnp.dot(oa_ref[0], wgo_ref[...], preferred_element_type=F32)
    yb = jnp.dot(cb_ref[0], wco_ref[...], preferred_element_type=F32)
    m = _sigmoid(ga_ref[0]) * ya + _sigmoid(gb_ref[0]) * yb
    mix = jnp.dot(m.astype(BF16), wout_ref[...], preferred_element_type=F32)
    x1 = x_ref[0] + gt1_ref[0] * (_rms(mix) * gpost_ref[...])
    x1_ref[0] = x1
    h2 = _rms(x1) * gpre_ref[...]
    h2_ref[0] = (h2 * (1.0 + sc2_ref[0]) + sh2_ref[0]).astype(h2_ref.dtype)


def _merge(oa, cb, z, x, gt1, sc2, sh2, g_post, g_pre, w_go, w_co, w_out, tt):
    bsz, t, d = x.shape
    ch = cb.shape[-1]
    tok = lambda w: pl.BlockSpec((1, tt, w), lambda b, i: (b, i, 0))
    return pl.pallas_call(
        _merge_kernel,
        grid=(bsz, t // tt),
        in_specs=[tok(d), tok(ch),
                  pl.BlockSpec((1, tt, d), lambda b, i: (b, i, 4)),
                  pl.BlockSpec((1, tt, d), lambda b, i: (b, i, 5)),
                  tok(d), _mod_spec(gt1, tt), _mod_spec(sc2, tt), _mod_spec(sh2, tt),
                  _const_spec((1, d)), _const_spec((1, d)),
                  _const_spec(w_go.shape), _const_spec(w_co.shape), _const_spec(w_out.shape)],
        out_specs=[tok(d), tok(d)],
        out_shape=[jax.ShapeDtypeStruct((bsz, t, d), F32),
                   jax.ShapeDtypeStruct((bsz, t, d), BF16)],
        compiler_params=_params("parallel", "parallel"),
        name="merge",
    )(oa, cb, z, z, x, gt1, sc2, sh2, g_post, g_pre, w_go, w_co, w_out)


def _keyfold_kernel(ks_ref, wq_ref, o_ref):
    o_ref[...] = lax.dot_general(ks_ref[0], wq_ref[...], (((1,), (1,)), ((), ())),
                                 preferred_element_type=F32,
                                 precision=lax.Precision.HIGHEST).astype(o_ref.dtype)


def _keyfold(k_sub, w_pq):
    d = w_pq.shape[0]
    _, nk, dq = k_sub.shape
    nblk = w_pq.shape[1] // dq
    return pl.pallas_call(
        _keyfold_kernel,
        grid=(nblk,),
        in_specs=[pl.BlockSpec((1, nk, dq), lambda j: (j % 2, 0, 0)),
                  pl.BlockSpec((d, dq), lambda j: (0, j))],
        out_specs=pl.BlockSpec((nk, d), lambda j: (j, 0)),
        out_shape=jax.ShapeDtypeStruct((nblk * nk, d), BF16),
        compiler_params=_params("arbitrary"),
        name="keyfold",
    )(k_sub, w_pq)


def _top16(s):
    rows, n = s.shape
    rid = lax.broadcasted_iota(jnp.int32, s.shape, 0).astype(F32)
    sub = lax.broadcasted_iota(jnp.int32, (SUBLANES, n), 0).astype(F32)
    vals, idxs = [], []
    for _ in range(PEER_TOPK):
        nodes = [(s[g:g + SUBLANES], sub + float(g)) for g in range(0, rows, SUBLANES)]
        while len(nodes) > 1:
            merged = []
            for (va, ra), (vb, rb) in zip(nodes[0::2], nodes[1::2]):
                left = va >= vb
                merged.append((jnp.where(left, va, vb), jnp.where(left, ra, rb)))
            nodes = merged
        v8, r8 = nodes[0]
        m = jnp.max(v8, axis=0, keepdims=True)
        am = jnp.min(jnp.where(v8 == m, r8, float(rows)), axis=0, keepdims=True)
        vals.append(m)
        idxs.append(am)
        s = jnp.where(rid == am, -jnp.inf, s)
    return vals, idxs


def _stack_rows(rows_list, n):
    k = len(rows_list)
    rid = lax.broadcasted_iota(jnp.int32, (k, n), 0)
    out = jnp.zeros((k, n), F32)
    for r, v in enumerate(rows_list):
        out = jnp.where(rid == r, v, out)
    return out


def _topk_columns(s1, s2):
    tn = s1.shape[1]
    nk = PEER_NKEYS
    v1, i1 = _top16(s1)
    v2, i2 = _top16(s2)
    v2a = _stack_rows(v2, tn)
    i2a = _stack_rows(i2, tn)
    v1a = _stack_rows(v1, tn)
    i1a = _stack_rows(i1, tn)
    sub = lax.broadcasted_iota(jnp.int32, (SUBLANES, tn), 0)
    cand, cidx = [], []
    for a in range(SUBLANES):
        lim = PEER_TOPK // (a + 1)
        for b0 in range(0, lim, SUBLANES):
            cv = v1[a] + v2a[b0:b0 + SUBLANES]
            ci = i1[a] * float(nk) + i2a[b0:b0 + SUBLANES]
            if lim - b0 < SUBLANES:
                cv = jnp.where(sub < lim - b0, cv, -jnp.inf)
            cand.append(cv)
            cidx.append(ci)
    cand.append(v1a[SUBLANES:] + v2[0])
    cidx.append(i1a[SUBLANES:] * float(nk) + i2[0])
    cand = jnp.concatenate(cand, axis=0)
    cidx = jnp.concatenate(cidx, axis=0)
    rid = lax.broadcasted_iota(jnp.int32, cand.shape, 0).astype(F32)
    sc, ids = [], []
    for _ in range(PEER_TOPK):
        m = jnp.max(cand, axis=0, keepdims=True)
        am = jnp.min(jnp.where(cand == m, rid, float(cand.shape[0])), axis=0, keepdims=True)
        hit = rid == am
        sc.append(m)
        ids.append(jnp.sum(jnp.where(hit, cidx, 0.0), axis=0, keepdims=True))
        cand = jnp.where(hit, -jnp.inf, cand)
    sca = _stack_rows(sc, tn)
    ida = _stack_rows(ids, tn)
    e = jnp.exp(sca - sc[0])
    wts = e / jnp.sum(e, axis=0, keepdims=True)
    ia = jnp.floor(ida * (1.0 / nk))
    return wts, ia, ida - ia * float(nk)


def _topk_kernel(h_ref, tab_ref, w_ref, a_ref, b_ref, s_scr, w_scr, a_scr, b_scr):
    hd = pl.program_id(1)
    tn = h_ref.shape[0]
    nk = PEER_NKEYS
    s_scr[...] = lax.dot_general(tab_ref[...], h_ref[...], (((1,), (1,)), ((), ())),
                                 preferred_element_type=F32)
    r0 = pl.multiple_of(hd * PEER_TOPK, PEER_TOPK)

    width = min(tn, 2 * LANES)

    def lane_block(c, carry):
        cols = pl.ds(pl.multiple_of(c * width, width), width)
        wts, ia, ib = _topk_columns(s_scr[0:nk, cols], s_scr[nk:2 * nk, cols])
        w_scr[pl.ds(r0, PEER_TOPK), cols] = wts
        a_scr[pl.ds(r0, PEER_TOPK), cols] = ia
        b_scr[pl.ds(r0, PEER_TOPK), cols] = ib
        return carry

    lax.fori_loop(0, tn // width, lane_block, 0)

    @pl.when(hd == pl.num_programs(1) - 1)
    def _():
        w_ref[...] = w_scr[...].T
        a_ref[...] = a_scr[...].T
        b_ref[...] = b_scr[...].T


def _topk(h2, table, tn):
    n, d = h2.shape
    nsel = PEER_HEADS * PEER_TOPK
    out = jax.ShapeDtypeStruct((n, nsel), F32)
    ospec = pl.BlockSpec((tn, nsel), lambda i, h: (i, 0))
    return pl.pallas_call(
        _topk_kernel,
        grid=(n // tn, PEER_HEADS),
        in_specs=[pl.BlockSpec((tn, d), lambda i, h: (i, 0)),
                  pl.BlockSpec((2 * PEER_NKEYS, d), lambda i, h: (h, 0))],
        out_specs=[ospec, ospec, ospec],
        out_shape=[out, out, out],
        scratch_shapes=[pltpu.VMEM((2 * PEER_NKEYS, tn), F32)] + [pltpu.VMEM((nsel, tn), F32)] * 3,
        compiler_params=_params("parallel", "arbitrary"),
        name="topk",
    )(h2, table)


def _act_kernel(h_ref, u_ref, w_ref, a_ref, b_ref, coef_ref, sel_scr, s_scr):
    k = pl.program_id(1)
    nb = u_ref.shape[0] // LANES

    @pl.when(k == 0)
    def _():
        sel_scr[...] = jnp.zeros(sel_scr.shape, F32)
        s_scr[...] = jnp.zeros(s_scr.shape, F32)

    ia = a_ref[...]
    ib = b_ref[...].astype(jnp.int32)

    def gather(key0, sel):
        for j in range(nb):
            g = jnp.take_along_axis(s_scr[:, j * LANES:(j + 1) * LANES], ib, axis=1)
            sel = jnp.where(ia == key0 + float(j), g, sel)
        return sel

    base = (k * nb).astype(F32)
    sel = gather(base - float(nb), sel_scr[...])
    s_scr[...] = lax.dot_general(h_ref[...], u_ref[...].astype(BF16), (((1,), (1,)), ((), ())),
                                 preferred_element_type=F32)
    sel_scr[...] = sel

    @pl.when(k == pl.num_programs(1) - 1)
    def _():
        x = gather(base, sel)
        act = 0.5 * x * (1.0 + lax.erf(x * (2.0 ** -0.5)))
        coef_ref[...] = w_ref[...] * act


def _expert_act(h2, u_bf, wts, ia, ib, tn, te):
    n, d = h2.shape
    ne = u_bf.shape[0]
    nsel = wts.shape[1]
    sspec = pl.BlockSpec((tn, nsel), lambda i, k: (i, 0))
    return pl.pallas_call(
        _act_kernel,
        grid=(n // tn, ne // te),
        in_specs=[pl.BlockSpec((tn, d), lambda i, k: (i, 0)),
                  pl.BlockSpec((te, d), lambda i, k: (k, 0)),
                  sspec, sspec, sspec],
        out_specs=sspec,
        out_shape=jax.ShapeDtypeStruct((n, nsel), F32),
        scratch_shapes=[pltpu.VMEM((tn, nsel), F32), pltpu.VMEM((tn, te), F32)],
        compiler_params=_params("parallel", "arbitrary"),
        name="expert_act",
    )(h2, u_bf, wts, ia, ib)


def _mixmat_kernel(*refs, cast_values):
    if cast_values:
        coef_ref, a_ref, b_ref, v_ref, p_ref, vb_ref, p_scr = refs
        vb_ref[...] = v_ref[...].astype(vb_ref.dtype)
    else:
        coef_ref, a_ref, b_ref, p_ref, p_scr = refs
    tn = coef_ref.shape[0]
    nk = PEER_NKEYS
    kid = lax.broadcasted_iota(jnp.int32, (nk, coef_ref.shape[1]), 0).astype(F32).astype(BF16)
    one = jnp.ones((nk, coef_ref.shape[1]), BF16)
    zero = jnp.zeros((nk, coef_ref.shape[1]), BF16)

    unroll = 4 * SUBLANES

    def group(g, carry):
        t0 = pl.multiple_of(g * unroll, unroll)
        for s in range(unroll):
            c = coef_ref[pl.ds(t0 + s, 1), :].astype(BF16)
            ia = a_ref[pl.ds(t0 + s, 1), :].astype(BF16)
            ib = b_ref[pl.ds(t0 + s, 1), :].astype(BF16)
            lhs = jnp.where(ia == kid, c, zero)
            rhs = jnp.where(ib == kid, one, zero)
            row0 = (g * (unroll // SUBLANES) + s // SUBLANES) * (nk * SUBLANES) + s % SUBLANES
            p_scr[pl.ds(row0, nk, stride=SUBLANES), :] = lax.dot_general(
                lhs, rhs, (((1,), (1,)), ((), ())), preferred_element_type=F32)
        return carry

    lax.fori_loop(0, tn // unroll, group, 0)

    def emit(rr, carry):
        lo = pl.multiple_of(rr * (2 * nk * SUBLANES), SUBLANES)
        hi = lo + nk * SUBLANES
        row = pl.multiple_of(rr * 2 * SUBLANES, 2 * SUBLANES)
        for j in range(nk):
            tile = jnp.concatenate([p_scr[pl.ds(lo + j * SUBLANES, SUBLANES), :],
                                    p_scr[pl.ds(hi + j * SUBLANES, SUBLANES), :]], axis=0)
            p_ref[pl.ds(row, 2 * SUBLANES), j * nk:(j + 1) * nk] = tile.astype(p_ref.dtype)
        return carry

    lax.fori_loop(0, tn // (2 * SUBLANES), emit, 0)


def _mixmat(coef, ia, ib, tn, v_exp=None):
    n, nsel = coef.shape
    ne = PEER_NKEYS * PEER_NKEYS
    steps = n // tn
    sspec = pl.BlockSpec((tn, nsel), lambda i: (i, 0))
    in_specs, out_specs = [sspec, sspec, sspec], [pl.BlockSpec((tn, ne), lambda i: (i, 0))]
    out_shape, args = [jax.ShapeDtypeStruct((n, ne), BF16)], [coef, ia, ib]
    if v_exp is not None:
        d = v_exp.shape[1]
        vspec = pl.BlockSpec((ne // steps, d), lambda i: (i, 0))
        in_specs.append(vspec)
        out_specs.append(vspec)
        out_shape.append(jax.ShapeDtypeStruct((ne, d), BF16))
        args.append(v_exp)
    out = pl.pallas_call(
        functools.partial(_mixmat_kernel, cast_values=v_exp is not None),
        grid=(steps,),
        in_specs=in_specs,
        out_specs=out_specs,
        out_shape=out_shape,
        scratch_shapes=[pltpu.VMEM((tn * PEER_NKEYS, PEER_NKEYS), F32)],
        compiler_params=_params("parallel"),
        name="mixmat",
    )(*args)
    return out if v_exp is not None else (out[0], None)


def _value_kernel(p_ref, v_ref, x1_ref, gt2_ref, g_ref, y_ref, acc_scr):
    k = pl.program_id(2)

    @pl.when(k == 0)
    def _():
        acc_scr[...] = jnp.zeros(acc_scr.shape, F32)

    acc_scr[...] += jnp.dot(p_ref[0], v_ref[...], preferred_element_type=F32)

    @pl.when(k == pl.num_programs(2) - 1)
    def _():
        y_ref[0] = x1_ref[0] + gt2_ref[0] * (_rms(acc_scr[...]) * g_ref[...])


def _value(p, v_bf, x1, gt2, g_post, tt, tk):
    bsz, t, d = x1.shape
    ne = v_bf.shape[0]
    return pl.pallas_call(
        _value_kernel,
        grid=(bsz, t // tt, ne // tk),
        in_specs=[pl.BlockSpec((1, tt, tk), lambda b, i, k: (b, i, k)),
                  pl.BlockSpec((tk, d), lambda b, i, k: (k, 0)),
                  pl.BlockSpec((1, tt, d), lambda b, i, k: (b, i, 0),
                               pipeline_mode=pl.Buffered(1)),
                  _mod_spec(gt2, tt),
                  pl.BlockSpec((1, d), lambda b, i, k: (0, 0))],
        out_specs=pl.BlockSpec((1, tt, d), lambda b, i, k: (b, i, 0),
                               pipeline_mode=pl.Buffered(1)),
        out_shape=jax.ShapeDtypeStruct((bsz, t, d), F32),
        scratch_shapes=[pltpu.VMEM((tt, d), F32)],
        compiler_params=_params("parallel", "parallel", "arbitrary"),
        name="value",
    )(p, v_bf, x1, gt2, g_post)


def _layer(x, mods, s0, cache, wts, *, rec_shape, tt, v_bf=None):
    sh1, sc1, gt1, sh2, sc2, gt2 = mods
    bsz, t, d = x.shape
    z, zlr = _inproj(x, sc1, sh1, wts["g_pre_mix"], wts["w_rows"], wts["w_lr"], wts["lr0"], tt)
    rs, rt = rec_shape
    zr = z.reshape(rs, rt, z.shape[-1])
    oa, s_new = _gla(zr, zlr.reshape(rs, rt, LANES), wts["w_au"], wts["b_alpha"],
                     wts["g_gla_out"], s0)
    cb, c_new = _conv(zr, cache, wts["w_dw"], wts["b_dw"], wts["g_conv_ln"], wts["b_conv_ln"],
                      min(rt, 256))
    x1, h2 = _merge(oa.reshape(bsz, t, -1), cb.reshape(bsz, t, -1), z, x, gt1, sc2, sh2,
                    wts["g_post_mix"], wts["g_pre_ffn"], wts["w_gla_o"], wts["w_conv_o"],
                    wts["w_out"], min(t, 256))
    n = bsz * t
    h2f = h2.reshape(n, d)
    pw, ia, ib = _topk(h2f, wts["key_table"], min(n, 512))
    coef = _expert_act(h2f, wts["u_exp"], pw, ia, ib, min(n, 1024), 1024)
    p, v_new = _mixmat(coef, ia, ib, 128, None if v_bf is not None else wts["v_exp"])
    v_bf = v_bf if v_bf is not None else v_new
    y = _value(p.reshape(bsz, t, -1), v_bf, x1, gt2, wts["g_post_ffn"], min(t, 512), 2048)
    return y, s_new, c_new, v_bf


def kernel(x_prompt, x_sample, state_gla, cache_conv, c_prompt, c_sample, w_ada, b_ada, g_pre_mix,
           g_post_mix, w_in, w_alpha_up, b_alpha, g_gla_out, w_gla_o, w_dw, b_dw, g_conv_ln,
           b_conv_ln, w_conv_o, w_out, g_pre_ffn, g_post_ffn, w_pq, k_sub1, k_sub2, u_exp, v_exp):
    depth = w_ada.shape[0]
    bp, tp, d = x_prompt.shape
    bs, ts, _ = x_sample.shape
    nh, dk, dv = state_gla.shape[2:]
    xp, xs = x_prompt, x_sample.reshape(1, bs * ts, d)
    outs = [[], [], [], []]
    for l in range(depth):
        lr0 = 2 * nh * dk + 2 * nh * dv
        wl = w_in[l].T
        row = lambda a: a[l].reshape(1, -1)
        wts = dict(
            w_rows=wl, lr0=lr0,
            w_lr=jnp.pad(wl[lr0:lr0 + GLA_GATE_RANK],
                         ((0, LANES - GLA_GATE_RANK), (0, 0))).astype(BF16),
            w_au=jnp.pad(w_alpha_up[l], ((0, LANES - GLA_GATE_RANK), (0, 0))).astype(BF16),
            b_alpha=row(b_alpha), g_gla_out=row(g_gla_out), g_pre_mix=row(g_pre_mix),
            g_post_mix=row(g_post_mix), g_pre_ffn=row(g_pre_ffn), g_post_ffn=row(g_post_ffn),
            w_dw=jnp.broadcast_to(w_dw[l][:, None, :], (CONV_WIDTH, SUBLANES, w_dw.shape[-1])),
            b_dw=row(b_dw), g_conv_ln=row(g_conv_ln), b_conv_ln=row(b_conv_ln),
            w_gla_o=w_gla_o[l].astype(BF16), w_conv_o=w_conv_o[l].astype(BF16),
            w_out=w_out[l].astype(BF16),
            key_table=_keyfold(jnp.stack([k_sub1[l], k_sub2[l]]), w_pq[l]),
            u_exp=u_exp[l], v_exp=v_exp[l],
        )
        c_all = jnp.concatenate(
            [c_prompt, c_sample, jnp.zeros((-(bp + bs) % SUBLANES, d), F32)], axis=0)
        mod = _ada(c_all, w_ada[l], b_ada[l])
        mods_p = [m.reshape(bp, 1, d) for m in jnp.split(mod[:bp], 6, axis=-1)]
        mods_s = [jnp.repeat(m, ts, axis=0).reshape(1, bs * ts, d)
                  for m in jnp.split(mod[bp:bp + bs], 6, axis=-1)]
        s0_p = jnp.zeros((bp, nh, dk, dv), F32)
        cache0_p = jnp.zeros((bp, CONV_WIDTH - 1, cache_conv.shape[-1]), F32)
        xp, s_p, cb_p, v_bf = _layer(xp, mods_p, s0_p, cache0_p, wts, rec_shape=(bp, tp), tt=1024)
        xs, s_s, cb_s, _ = _layer(xs, mods_s, state_gla[l], cache_conv[l], wts,
                                  rec_shape=(bs, ts), tt=bs * ts, v_bf=v_bf)
        for o, v in zip(outs, (s_p, cb_p, s_s, cb_s)):
            o.append(v)
    return (xp, xs.reshape(bs, ts, d), jnp.stack(outs[0]), jnp.stack(outs[1]),
            jnp.stack(outs[2]), jnp.stack(outs[3]))
```

```python
import functools

import jax
import jax.numpy as jnp
from jax import lax
from jax.experimental import pallas as pl
from jax.experimental.pallas import tpu as pltpu

F32 = jnp.float32
BF16 = jnp.bfloat16

CHUNK = 64
GLA_HEADS = 4
GLA_GATE_RANK = 16
GLA_GATE_NORM = 16.0
CONV_WIDTH = 31
PEER_HEADS = 8
PEER_NKEYS = 128
PEER_TOPK = 16
EPS = 1e-6

LANES = 128
SUBLANES = 8
MXU_WIDTH = 256
CONV_HALO = 32
VMEM_LIMIT = 52 * 1024 * 1024


def _params(*sem):
    return pltpu.CompilerParams(dimension_semantics=sem, vmem_limit_bytes=VMEM_LIMIT)


def _const_spec(shape):
    nd = len(shape)
    return pl.BlockSpec(shape, lambda *_: (0,) * nd, pipeline_mode=pl.Buffered(1))


def _rms(x):
    return x * lax.rsqrt(jnp.mean(x * x, axis=-1, keepdims=True) + EPS)


def _sigmoid(x):
    return 1.0 / (1.0 + jnp.exp(-x))


def _silu(x):
    return x * _sigmoid(x)


def _ada_kernel(c_ref, w_ref, b_ref, o_ref):
    c = c_ref[...]
    o_ref[...] = jnp.dot(_silu(c).astype(BF16), w_ref[...].astype(BF16),
                         preferred_element_type=F32) + b_ref[...]


def _ada(c, w_ada, b_ada):
    g, d = c.shape
    n = w_ada.shape[1]
    tc = 1024
    return pl.pallas_call(
        _ada_kernel,
        grid=(n // tc,),
        in_specs=[pl.BlockSpec((g, d), lambda j: (0, 0)),
                  pl.BlockSpec((d, tc), lambda j: (0, j)),
                  pl.BlockSpec((1, tc), lambda j: (0, j))],
        out_specs=pl.BlockSpec((g, tc), lambda j: (0, j)),
        out_shape=jax.ShapeDtypeStruct((g, n), F32),
        compiler_params=_params("arbitrary"),
        name="ada",
    )(c, w_ada, b_ada.reshape(1, n))


def _inproj_kernel(x_ref, sc_ref, sh_ref, g_ref, w_ref, wlr_ref, z_ref, zlr_ref, h_scr):
    @pl.when(pl.program_id(2) == 0)
    def _():
        h = _rms(x_ref[0]) * g_ref[...]
        h = (h * (1.0 + sc_ref[0]) + sh_ref[0]).astype(BF16)
        h_scr[...] = h
        zlr_ref[0] = lax.dot_general(h, wlr_ref[...], (((1,), (1,)), ((), ())),
                                     preferred_element_type=F32)

    z_ref[0] = lax.dot_general(h_scr[...], w_ref[...].astype(BF16), (((1,), (1,)), ((), ())),
                               preferred_element_type=F32)


def _mod_spec(m, tt):
    d = m.shape[-1]
    if m.shape[1] == 1:
        return pl.BlockSpec((1, 1, d), lambda b, i, *_: (b, 0, 0))
    return pl.BlockSpec((1, tt, d), lambda b, i, *_: (b, i, 0))


def _inproj(x, sc, sh, g, w_rows, w_lr, lr0, tt):
    bsz, t, d = x.shape
    n = w_rows.shape[0] - GLA_GATE_RANK
    tc = 1024
    assert lr0 % tc == 0 and n == 2 * lr0
    wrow = lambda b, i, j: (
        pl.multiple_of(j * tc + (j // (lr0 // tc)) * GLA_GATE_RANK, GLA_GATE_RANK), 0)
    return pl.pallas_call(
        _inproj_kernel,
        grid=(bsz, t // tt, n // tc),
        in_specs=[pl.BlockSpec((1, tt, d), lambda b, i, j: (b, i, 0)),
                  _mod_spec(sc, tt), _mod_spec(sh, tt),
                  pl.BlockSpec((1, d), lambda b, i, j: (0, 0)),
                  pl.BlockSpec((pl.Element(tc), pl.Element(d)), wrow),
                  pl.BlockSpec((LANES, d), lambda b, i, j: (0, 0))],
        out_specs=[pl.BlockSpec((1, tt, tc), lambda b, i, j: (b, i, j)),
                   pl.BlockSpec((1, tt, LANES), lambda b, i, j: (b, i, 0))],
        out_shape=[jax.ShapeDtypeStruct((bsz, t, n), F32),
                   jax.ShapeDtypeStruct((bsz, t, LANES), F32)],
        scratch_shapes=[pltpu.VMEM((tt, d), BF16)],
        compiler_params=_params("parallel", "parallel", "arbitrary"),
        name="inproj",
    )(x, sc, sh, g, w_rows, w_lr)


def _gla_kernel(q_ref, k_ref, v_ref, r_ref, lr_ref, wau_ref, ba_ref, gg_ref, s0_ref,
                o_ref, s_ref, st_scr, *, dk, dv):
    c = pl.program_id(1)
    streams, rows = q_ref.shape[0], q_ref.shape[1]

    @pl.when(c == 0)
    def _():
        for s in range(streams):
            for h in range(GLA_HEADS):
                st_scr[s, h] = s0_ref[s, h].T

    for s in range(streams):
        _gla_chunk(s, q_ref, k_ref, v_ref, r_ref, lr_ref, wau_ref, ba_ref, gg_ref, o_ref, st_scr,
                   dk=dk, dv=dv)

    @pl.when(c == pl.num_programs(1) - 1)
    def _():
        for s in range(streams):
            for h in range(GLA_HEADS):
                s_ref[s, h] = st_scr[s, h].T


def _gla_chunk(s, q_ref, k_ref, v_ref, r_ref, lr_ref, wau_ref, ba_ref, gg_ref, o_ref, st_scr,
               *, dk, dv):
    rows = q_ref.shape[1]
    pre = jnp.dot(lr_ref[s].astype(BF16), wau_ref[...], preferred_element_type=F32) + ba_ref[...]
    log_a = (jnp.minimum(pre, 0.0) - jnp.log1p(jnp.exp(-jnp.abs(pre)))) * (1.0 / GLA_GATE_NORM)
    row = lax.broadcasted_iota(jnp.int32, log_a.shape, 0)
    b = log_a
    shift = 1
    while shift < rows:
        b = b + jnp.where(row >= shift, pltpu.roll(b, shift, axis=0), 0.0)
        shift *= 2
    b_last = b[rows - 1:rows, :]
    e_pos = jnp.exp(b)
    q_in = q_ref[s] * (dk ** -0.5) * e_pos
    kk = k_ref[s]
    k_in = kk * jnp.exp(-b)
    k_out = kk * jnp.exp(b_last - b)
    e_last = jnp.exp(b_last)
    causal = (lax.broadcasted_iota(jnp.int32, (rows, rows), 0)
              >= lax.broadcasted_iota(jnp.int32, (rows, rows), 1))
    gg = gg_ref[...]
    for h in range(GLA_HEADS):
        ks = slice(h * dk, (h + 1) * dk)
        vs = slice(h * dv, (h + 1) * dv)
        qh = q_in[:, ks].astype(BF16)
        vh = v_ref[s, :, vs].astype(BF16)
        st = st_scr[s, h]
        att = lax.dot_general(qh, k_in[:, ks].astype(BF16), (((1,), (1,)), ((), ())),
                              preferred_element_type=F32)
        att = jnp.where(causal, att, 0.0).astype(BF16)
        o = jnp.dot(att, vh, preferred_element_type=F32)
        o = o + lax.dot_general(qh, st.astype(BF16), (((1,), (1,)), ((), ())),
                                preferred_element_type=F32)
        st_scr[s, h] = st * e_last[:, ks] + lax.dot_general(
            vh, k_out[:, ks].astype(BF16), (((0,), (0,)), ((), ())), preferred_element_type=F32)
        o = _rms(o) * gg * _silu(r_ref[s, :, vs])
        o_ref[s, :, vs] = o.astype(o_ref.dtype)


def _gla(z, zlr, w_au, b_alpha, g_gla_out, s0):
    bsz, t, _ = z.shape
    _, nh, dk, dv = s0.shape
    rows = min(CHUNK, t)
    sb = 2 if bsz % 2 == 0 else 1
    kern = functools.partial(_gla_kernel, dk=dk, dv=dv)
    return pl.pallas_call(
        kern,
        grid=(bsz // sb, t // rows),
        in_specs=[pl.BlockSpec((sb, rows, nh * dk), lambda b, c: (b, c, 0)),
                  pl.BlockSpec((sb, rows, nh * dk), lambda b, c: (b, c, 1)),
                  pl.BlockSpec((sb, rows, nh * dv), lambda b, c: (b, c, 1)),
                  pl.BlockSpec((sb, rows, nh * dv), lambda b, c: (b, c, 2)),
                  pl.BlockSpec((sb, rows, LANES), lambda b, c: (b, c, 0)),
                  pl.BlockSpec((LANES, nh * dk), lambda b, c: (0, 0)),
                  pl.BlockSpec((1, nh * dk), lambda b, c: (0, 0)),
                  pl.BlockSpec((1, dv), lambda b, c: (0, 0)),
                  pl.BlockSpec((sb, nh, dk, dv), lambda b, c: (b, 0, 0, 0))],
        out_specs=[pl.BlockSpec((sb, rows, nh * dv), lambda b, c: (b, c, 0)),
                   pl.BlockSpec((sb, nh, dk, dv), lambda b, c: (b, 0, 0, 0))],
        out_shape=[jax.ShapeDtypeStruct((bsz, t, nh * dv), BF16),
                   jax.ShapeDtypeStruct((bsz, nh, dk, dv), F32)],
        scratch_shapes=[pltpu.VMEM((sb, nh, dv, dk), F32)],
        compiler_params=_params("parallel", "arbitrary"),
        name="gla",
    )(z, z, z, z, zlr, w_au, b_alpha, g_gla_out, s0)


def _conv_kernel(a_ref, g_ref, cache_ref, wdw_ref, bdw_ref, gln_ref, bln_ref,
                 y_ref, newc_ref, shift_scr, conv_scr):
    i = pl.program_id(1)
    tt = a_ref.shape[1]
    hist = CONV_WIDTH - 1
    pad = CONV_HALO - hist
    win_scr = shift_scr.at[0]

    @pl.when(i == 0)
    def _():
        win_scr[0:pad, :] = jnp.zeros((pad, win_scr.shape[1]), F32)
        win_scr[pad:CONV_HALO, :] = cache_ref[0]

    win_scr[CONV_HALO:CONV_HALO + tt, :] = a_ref[0] * _sigmoid(g_ref[0])
    full = win_scr[...]
    for s in range(1, SUBLANES):
        shift_scr[s] = pltpu.roll(full, full.shape[0] - s, axis=0)
    bdw = bdw_ref[...]
    gln = gln_ref[...]
    bln = bln_ref[...]

    rows = min(tt, 4 * SUBLANES)

    def chunk(ci, carry):
        r0 = pl.multiple_of(ci * rows, rows)
        acc = jnp.zeros((rows // SUBLANES, SUBLANES, win_scr.shape[1]), F32)
        for w in range(CONV_WIDTH):
            q, s = divmod(pad + w, SUBLANES)
            x = shift_scr[s, pl.ds(r0 + q * SUBLANES, rows), :]
            acc = acc + x.reshape(rows // SUBLANES, SUBLANES, x.shape[1]) * wdw_ref[w]
        conv_scr[pl.ds(r0, rows), :] = acc.reshape(rows, win_scr.shape[1])
        return carry

    lax.fori_loop(0, tt // rows, chunk, 0)
    acc = conv_scr[...] + bdw
    mu = jnp.mean(acc, axis=-1, keepdims=True)
    xc = acc - mu
    yn = xc * lax.rsqrt(jnp.mean(xc * xc, axis=-1, keepdims=True) + EPS) * gln + bln
    y_ref[0] = _silu(yn).astype(y_ref.dtype)

    @pl.when(i == pl.num_programs(1) - 1)
    def _():
        newc_ref[0] = win_scr[tt + pad:tt + CONV_HALO, :]

    tail = win_scr[tt:tt + CONV_HALO, :]
    win_scr[0:CONV_HALO, :] = tail


def _conv(z, cache, w_dw, b_dw, g_ln, b_ln, tt):
    bsz, t, _ = z.shape
    _, hist, ch = cache.shape
    return pl.pallas_call(
        _conv_kernel,
        grid=(bsz, t // tt),
        in_specs=[pl.BlockSpec((1, tt, ch), lambda b, i: (b, i, 6)),
                  pl.BlockSpec((1, tt, ch), lambda b, i: (b, i, 7)),
                  pl.BlockSpec((1, hist, ch), lambda b, i: (b, 0, 0)),
                  pl.BlockSpec((CONV_WIDTH, SUBLANES, ch), lambda b, i: (0, 0, 0)),
                  pl.BlockSpec((1, ch), lambda b, i: (0, 0)),
                  pl.BlockSpec((1, ch), lambda b, i: (0, 0)),
                  pl.BlockSpec((1, ch), lambda b, i: (0, 0))],
        out_specs=[pl.BlockSpec((1, tt, ch), lambda b, i: (b, i, 0)),
                   pl.BlockSpec((1, hist, ch), lambda b, i: (b, 0, 0))],
        out_shape=[jax.ShapeDtypeStruct((bsz, t, ch), BF16),
                   jax.ShapeDtypeStruct((bsz, hist, ch), F32)],
        scratch_shapes=[pltpu.VMEM((SUBLANES, tt + CONV_HALO, ch), F32), pltpu.VMEM((tt, ch), F32)],
        compiler_params=_params("parallel", "arbitrary"),
        name="conv",
    )(z, z, cache, w_dw, b_dw, g_ln, b_ln)


def _merge_kernel(oa_ref, cb_ref, ga_ref, gb_ref, x_ref, gt1_ref, sc2_ref, sh2_ref,
                  gpost_ref, gpre_ref, wgo_ref, wco_ref, wout_ref, x1_ref, h2_ref):
    ya = jnp.dot(oa_ref[0], wgo_ref[...], preferred_element_type=F32)
    yb = jnp.dot(cb_ref[0], wco_ref[...], preferred_element_type=F32)
    m = _sigmoid(ga_ref[0]) * ya + _sigmoid(gb_ref[0]) * yb
    mix = jnp.dot(m.astype(BF16), wout_ref[...], preferred_element_type=F32)
    x1 = x_ref[0] + gt1_ref[0] * (_rms(mix) * gpost_ref[...])
    x1_ref[0] = x1
    h2 = _rms(x1) * gpre_ref[...]
    h2_ref[0] = (h2 * (1.0 + sc2_ref[0]) + sh2_ref[0]).astype(h2_ref.dtype)


def _merge(oa, cb, z, x, gt1, sc2, sh2, g_post, g_pre, w_go, w_co, w_out, tt):
    bsz, t, d = x.shape
    ch = cb.shape[-1]
    tok = lambda w: pl.BlockSpec((1, tt, w), lambda b, i: (b, i, 0))
    return pl.pallas_call(
        _merge_kernel,
        grid=(bsz, t // tt),
        in_specs=[tok(d), tok(ch),
                  pl.BlockSpec((1, tt, d), lambda b, i: (b, i, 4)),
                  pl.BlockSpec((1, tt, d), lambda b, i: (b, i, 5)),
                  tok(d), _mod_spec(gt1, tt), _mod_spec(sc2, tt), _mod_spec(sh2, tt),
                  _const_spec((1, d)), _const_spec((1, d)),
                  _const_spec(w_go.shape), _const_spec(w_co.shape), _const_spec(w_out.shape)],
        out_specs=[tok(d), tok(d)],
        out_shape=[jax.ShapeDtypeStruct((bsz, t, d), F32),
                   jax.ShapeDtypeStruct((bsz, t, d), BF16)],
        compiler_params=_params("parallel", "parallel"),
        name="merge",
    )(oa, cb, z, z, x, gt1, sc2, sh2, g_post, g_pre, w_go, w_co, w_out)


def _keyfold_kernel(ks_ref, wq_ref, o_ref):
    o_ref[...] = lax.dot_general(ks_ref[0].astype(BF16), wq_ref[...].astype(BF16),
                                 (((1,), (1,)), ((), ())),
                                 preferred_element_type=F32).astype(o_ref.dtype)


def _keyfold(k_sub, w_pq):
    d = w_pq.shape[0]
    _, nk, dq = k_sub.shape
    nblk = w_pq.shape[1] // dq
    return pl.pallas_call(
        _keyfold_kernel,
        grid=(nblk,),
        in_specs=[pl.BlockSpec((1, nk, dq), lambda j: (j % 2, 0, 0)),
                  pl.BlockSpec((d, dq), lambda j: (0, j))],
        out_specs=pl.BlockSpec((nk, d), lambda j: (j, 0)),
        out_shape=jax.ShapeDtypeStruct((nblk * nk, d), BF16),
        compiler_params=_params("arbitrary"),
        name="keyfold",
    )(k_sub, w_pq)


def _top16(s):
    rows, n = s.shape
    rid = lax.broadcasted_iota(jnp.int32, s.shape, 0).astype(F32)
    sub = lax.broadcasted_iota(jnp.int32, (SUBLANES, n), 0).astype(F32)
    vals, idxs = [], []
    for _ in range(PEER_TOPK):
        nodes = [(s[g:g + SUBLANES], sub + float(g)) for g in range(0, rows, SUBLANES)]
        while len(nodes) > 1:
            merged = []
            for (va, ra), (vb, rb) in zip(nodes[0::2], nodes[1::2]):
                left = va >= vb
                merged.append((jnp.where(left, va, vb), jnp.where(left, ra, rb)))
            nodes = merged
        v8, r8 = nodes[0]
        m = jnp.max(v8, axis=0, keepdims=True)
        am = jnp.min(jnp.where(v8 == m, r8, float(rows)), axis=0, keepdims=True)
        vals.append(m)
        idxs.append(am)
        s = jnp.where(rid == am, -jnp.inf, s)
    return vals, idxs


def _stack_rows(rows_list, n):
    k = len(rows_list)
    rid = lax.broadcasted_iota(jnp.int32, (k, n), 0)
    out = jnp.zeros((k, n), F32)
    for r, v in enumerate(rows_list):
        out = jnp.where(rid == r, v, out)
    return out


def _topk_columns(s1, s2):
    tn = s1.shape[1]
    nk = PEER_NKEYS
    v1, i1 = _top16(s1)
    v2, i2 = _top16(s2)
    v2a = _stack_rows(v2, tn)
    i2a = _stack_rows(i2, tn)
    v1a = _stack_rows(v1, tn)
    i1a = _stack_rows(i1, tn)
    sub = lax.broadcasted_iota(jnp.int32, (SUBLANES, tn), 0)
    cand, cidx = [], []
    for a in range(SUBLANES):
        lim = PEER_TOPK // (a + 1)
        for b0 in range(0, lim, SUBLANES):
            cv = v1[a] + v2a[b0:b0 + SUBLANES]
            ci = i1[a] * float(nk) + i2a[b0:b0 + SUBLANES]
            if lim - b0 < SUBLANES:
                cv = jnp.where(sub < lim - b0, cv, -jnp.inf)
            cand.append(cv)
            cidx.append(ci)
    cand.append(v1a[SUBLANES:] + v2[0])
    cidx.append(i1a[SUBLANES:] * float(nk) + i2[0])
    cand = jnp.concatenate(cand, axis=0)
    cidx = jnp.concatenate(cidx, axis=0)
    rid = lax.broadcasted_iota(jnp.int32, cand.shape, 0).astype(F32)
    sc, ids = [], []
    for _ in range(PEER_TOPK):
        m = jnp.max(cand, axis=0, keepdims=True)
        am = jnp.min(jnp.where(cand == m, rid, float(cand.shape[0])), axis=0, keepdims=True)
        hit = rid == am
        sc.append(m)
        ids.append(jnp.sum(jnp.where(hit, cidx, 0.0), axis=0, keepdims=True))
        cand = jnp.where(hit, -jnp.inf, cand)
    sca = _stack_rows(sc, tn)
    ida = _stack_rows(ids, tn)
    e = jnp.exp(sca - sc[0])
    wts = e / jnp.sum(e, axis=0, keepdims=True)
    ia = jnp.floor(ida * (1.0 / nk))
    return wts, ia, ida - ia * float(nk)


def _topk_kernel(h_ref, tab_ref, w_ref, a_ref, b_ref, s_scr, w_scr, a_scr, b_scr):
    hd = pl.program_id(1)
    tn = h_ref.shape[0]
    nk = PEER_NKEYS
    s_scr[...] = lax.dot_general(tab_ref[...], h_ref[...], (((1,), (1,)), ((), ())),
                                 preferred_element_type=F32)
    r0 = pl.multiple_of(hd * PEER_TOPK, PEER_TOPK)

    width = min(tn, 2 * LANES)

    def lane_block(c, carry):
        cols = pl.ds(pl.multiple_of(c * width, width), width)
        wts, ia, ib = _topk_columns(s_scr[0:nk, cols], s_scr[nk:2 * nk, cols])
        w_scr[pl.ds(r0, PEER_TOPK), cols] = wts
        a_scr[pl.ds(r0, PEER_TOPK), cols] = ia
        b_scr[pl.ds(r0, PEER_TOPK), cols] = ib
        return carry

    lax.fori_loop(0, tn // width, lane_block, 0)

    @pl.when(hd == pl.num_programs(1) - 1)
    def _():
        w_ref[...] = w_scr[...].T
        a_ref[...] = a_scr[...].T
        b_ref[...] = b_scr[...].T


def _topk(h2, table, tn):
    n, d = h2.shape
    nsel = PEER_HEADS * PEER_TOPK
    out = jax.ShapeDtypeStruct((n, nsel), F32)
    ospec = pl.BlockSpec((tn, nsel), lambda i, h: (i, 0))
    return pl.pallas_call(
        _topk_kernel,
        grid=(n // tn, PEER_HEADS),
        in_specs=[pl.BlockSpec((tn, d), lambda i, h: (i, 0)),
                  pl.BlockSpec((2 * PEER_NKEYS, d), lambda i, h: (h, 0))],
        out_specs=[ospec, ospec, ospec],
        out_shape=[out, out, out],
        scratch_shapes=[pltpu.VMEM((2 * PEER_NKEYS, tn), F32)] + [pltpu.VMEM((nsel, tn), F32)] * 3,
        compiler_params=_params("parallel", "arbitrary"),
        name="topk",
    )(h2, table)


def _act_kernel(h_ref, u_ref, w_ref, a_ref, b_ref, coef_ref, sel_scr, s_scr):
    k = pl.program_id(1)
    nb = u_ref.shape[0] // LANES

    @pl.when(k == 0)
    def _():
        sel_scr[...] = jnp.zeros(sel_scr.shape, F32)
        s_scr[...] = jnp.zeros(s_scr.shape, F32)

    ia = a_ref[...]
    ib = b_ref[...].astype(jnp.int32)

    def gather(key0, sel):
        for j in range(nb):
            g = jnp.take_along_axis(s_scr[:, j * LANES:(j + 1) * LANES], ib, axis=1)
            sel = jnp.where(ia == key0 + float(j), g, sel)
        return sel

    base = (k * nb).astype(F32)
    sel = gather(base - float(nb), sel_scr[...])
    s_scr[...] = lax.dot_general(h_ref[...], u_ref[...].astype(BF16), (((1,), (1,)), ((), ())),
                                 preferred_element_type=F32)
    sel_scr[...] = sel

    @pl.when(k == pl.num_programs(1) - 1)
    def _():
        x = gather(base, sel)
        act = 0.5 * x * (1.0 + lax.erf(x * (2.0 ** -0.5)))
        coef_ref[...] = w_ref[...] * act


def _expert_act(h2, u_bf, wts, ia, ib, tn, te):
    n, d = h2.shape
    ne = u_bf.shape[0]
    nsel = wts.shape[1]
    sspec = pl.BlockSpec((tn, nsel), lambda i, k: (i, 0))
    return pl.pallas_call(
        _act_kernel,
        grid=(n // tn, ne // te),
        in_specs=[pl.BlockSpec((tn, d), lambda i, k: (i, 0)),
                  pl.BlockSpec((te, d), lambda i, k: (k, 0)),
                  sspec, sspec, sspec],
        out_specs=sspec,
        out_shape=jax.ShapeDtypeStruct((n, nsel), F32),
        scratch_shapes=[pltpu.VMEM((tn, nsel), F32), pltpu.VMEM((tn, te), F32)],
        compiler_params=_params("parallel", "arbitrary"),
        name="expert_act",
    )(h2, u_bf, wts, ia, ib)


def _mixmat_kernel(*refs, ntables):
    coef_ref, a_ref, b_ref = refs[:3]
    p_ref, p_scr = refs[3 + ntables], refs[-1]
    for src, dst in zip(refs[3:3 + ntables], refs[4 + ntables:-1]):
        dst[...] = src[...].astype(dst.dtype)
    tn = coef_ref.shape[0]
    nk = PEER_NKEYS
    kid = lax.broadcasted_iota(jnp.int32, (nk, coef_ref.shape[1]), 0).astype(F32).astype(BF16)
    one = jnp.ones((nk, coef_ref.shape[1]), BF16)
    zero = jnp.zeros((nk, coef_ref.shape[1]), BF16)

    unroll = 4 * SUBLANES

    def group(g, carry):
        t0 = pl.multiple_of(g * unroll, unroll)
        for s in range(unroll):
            c = coef_ref[pl.ds(t0 + s, 1), :].astype(BF16)
            ia = a_ref[pl.ds(t0 + s, 1), :].astype(BF16)
            ib = b_ref[pl.ds(t0 + s, 1), :].astype(BF16)
            lhs = jnp.where(ia == kid, c, zero)
            rhs = jnp.where(ib == kid, one, zero)
            row0 = (g * (unroll // SUBLANES) + s // SUBLANES) * (nk * SUBLANES) + s % SUBLANES
            p_scr[pl.ds(row0, nk, stride=SUBLANES), :] = lax.dot_general(
                lhs, rhs, (((1,), (1,)), ((), ())), preferred_element_type=F32)
        return carry

    lax.fori_loop(0, tn // unroll, group, 0)

    def emit(rr, carry):
        lo = pl.multiple_of(rr * (2 * nk * SUBLANES), SUBLANES)
        hi = lo + nk * SUBLANES
        row = pl.multiple_of(rr * 2 * SUBLANES, 2 * SUBLANES)
        for j in range(nk):
            tile = jnp.concatenate([p_scr[pl.ds(lo + j * SUBLANES, SUBLANES), :],
                                    p_scr[pl.ds(hi + j * SUBLANES, SUBLANES), :]], axis=0)
            p_ref[pl.ds(row, 2 * SUBLANES), j * nk:(j + 1) * nk] = tile.astype(p_ref.dtype)
        return carry

    lax.fori_loop(0, tn // (2 * SUBLANES), emit, 0)


def _mixmat(coef, ia, ib, tn, tables=()):
    n, nsel = coef.shape
    ne = PEER_NKEYS * PEER_NKEYS
    steps = n // tn
    sspec = pl.BlockSpec((tn, nsel), lambda i: (i, 0))
    tspecs = [pl.BlockSpec((ne // steps, tab.shape[1]), lambda i: (i, 0)) for tab in tables]
    return pl.pallas_call(
        functools.partial(_mixmat_kernel, ntables=len(tables)),
        grid=(steps,),
        in_specs=[sspec, sspec, sspec] + tspecs,
        out_specs=[pl.BlockSpec((tn, ne), lambda i: (i, 0))] + tspecs,
        out_shape=[jax.ShapeDtypeStruct((n, ne), BF16)]
        + [jax.ShapeDtypeStruct(tab.shape, BF16) for tab in tables],
        scratch_shapes=[pltpu.VMEM((tn * PEER_NKEYS, PEER_NKEYS), F32)],
        compiler_params=_params("parallel"),
        name="mixmat",
    )(coef, ia, ib, *tables)


def _value_kernel(p_ref, v_ref, x1_ref, gt2_ref, g_ref, y_ref, acc_scr):
    k = pl.program_id(2)

    @pl.when(k == 0)
    def _():
        acc_scr[...] = jnp.zeros(acc_scr.shape, F32)

    acc_scr[...] += jnp.dot(p_ref[0], v_ref[...], preferred_element_type=F32)

    @pl.when(k == pl.num_programs(2) - 1)
    def _():
        y_ref[0] = x1_ref[0] + gt2_ref[0] * (_rms(acc_scr[...]) * g_ref[...])


def _value(p, v_bf, x1, gt2, g_post, tt, tk):
    bsz, t, d = x1.shape
    ne = v_bf.shape[0]
    return pl.pallas_call(
        _value_kernel,
        grid=(bsz, t // tt, ne // tk),
        in_specs=[pl.BlockSpec((1, tt, tk), lambda b, i, k: (b, i, k)),
                  pl.BlockSpec((tk, d), lambda b, i, k: (k, 0)),
                  pl.BlockSpec((1, tt, d), lambda b, i, k: (b, i, 0),
                               pipeline_mode=pl.Buffered(1)),
                  _mod_spec(gt2, tt),
                  pl.BlockSpec((1, d), lambda b, i, k: (0, 0))],
        out_specs=pl.BlockSpec((1, tt, d), lambda b, i, k: (b, i, 0),
                               pipeline_mode=pl.Buffered(1)),
        out_shape=jax.ShapeDtypeStruct((bsz, t, d), F32),
        scratch_shapes=[pltpu.VMEM((tt, d), F32)],
        compiler_params=_params("parallel", "parallel", "arbitrary"),
        name="value",
    )(p, v_bf, x1, gt2, g_post)


def _layer(x, mods, s0, cache, wts, *, rec_shape, tt, experts_bf=None):
    sh1, sc1, gt1, sh2, sc2, gt2 = mods
    bsz, t, d = x.shape
    z, zlr = _inproj(x, sc1, sh1, wts["g_pre_mix"], wts["w_rows"], wts["w_lr"], wts["lr0"], tt)
    rs, rt = rec_shape
    zr = z.reshape(rs, rt, z.shape[-1])
    oa, s_new = _gla(zr, zlr.reshape(rs, rt, LANES), wts["w_au"], wts["b_alpha"],
                     wts["g_gla_out"], s0)
    cb, c_new = _conv(zr, cache, wts["w_dw"], wts["b_dw"], wts["g_conv_ln"], wts["b_conv_ln"],
                      min(rt, 256))
    x1, h2 = _merge(oa.reshape(bsz, t, -1), cb.reshape(bsz, t, -1), z, x, gt1, sc2, sh2,
                    wts["g_post_mix"], wts["g_pre_ffn"], wts["w_gla_o"], wts["w_conv_o"],
                    wts["w_out"], min(t, 256))
    n = bsz * t
    h2f = h2.reshape(n, d)
    pw, ia, ib = _topk(h2f, wts["key_table"], min(n, 512))
    u_tab = wts["u_exp"] if experts_bf is None else experts_bf[0]
    coef = _expert_act(h2f, u_tab, pw, ia, ib, min(n, 1024), 1024)
    if experts_bf is None:
        p, *experts_bf = _mixmat(coef, ia, ib, 128, (wts["u_exp"], wts["v_exp"]))
    else:
        p, = _mixmat(coef, ia, ib, 128)
    y = _value(p.reshape(bsz, t, -1), experts_bf[1], x1, gt2, wts["g_post_ffn"],
               min(t, 512), 2048)
    return y, s_new, c_new, experts_bf


def kernel(x_prompt, x_sample, state_gla, cache_conv, c_prompt, c_sample, w_ada, b_ada, g_pre_mix,
           g_post_mix, w_in, w_alpha_up, b_alpha, g_gla_out, w_gla_o, w_dw, b_dw, g_conv_ln,
           b_conv_ln, w_conv_o, w_out, g_pre_ffn, g_post_ffn, w_pq, k_sub1, k_sub2, u_exp, v_exp):
    depth = w_ada.shape[0]
    bp, tp, d = x_prompt.shape
    bs, ts, _ = x_sample.shape
    nh, dk, dv = state_gla.shape[2:]
    xp, xs = x_prompt, x_sample.reshape(1, bs * ts, d)
    outs = [[], [], [], []]
    for l in range(depth):
        lr0 = 2 * nh * dk + 2 * nh * dv
        wl = w_in[l].T
        row = lambda a: a[l].reshape(1, -1)
        wts = dict(
            w_rows=wl, lr0=lr0,
            w_lr=jnp.pad(wl[lr0:lr0 + GLA_GATE_RANK],
                         ((0, LANES - GLA_GATE_RANK), (0, 0))).astype(BF16),
            w_au=jnp.pad(w_alpha_up[l], ((0, LANES - GLA_GATE_RANK), (0, 0))).astype(BF16),
            b_alpha=row(b_alpha), g_gla_out=row(g_gla_out), g_pre_mix=row(g_pre_mix),
            g_post_mix=row(g_post_mix), g_pre_ffn=row(g_pre_ffn), g_post_ffn=row(g_post_ffn),
            w_dw=jnp.broadcast_to(w_dw[l][:, None, :], (CONV_WIDTH, SUBLANES, w_dw.shape[-1])),
            b_dw=row(b_dw), g_conv_ln=row(g_conv_ln), b_conv_ln=row(b_conv_ln),
            w_gla_o=w_gla_o[l].astype(BF16), w_conv_o=w_conv_o[l].astype(BF16),
            w_out=w_out[l].astype(BF16),
            key_table=_keyfold(jnp.stack([k_sub1[l], k_sub2[l]]), w_pq[l]),
            u_exp=u_exp[l], v_exp=v_exp[l],
        )
        c_all = jnp.concatenate(
            [c_prompt, c_sample, jnp.zeros((-(bp + bs) % SUBLANES, d), F32)], axis=0)
        mod = _ada(c_all, w_ada[l], b_ada[l])
        mods_p = [m.reshape(bp, 1, d) for m in jnp.split(mod[:bp], 6, axis=-1)]
        mods_s = [jnp.repeat(m, ts, axis=0).reshape(1, bs * ts, d)
                  for m in jnp.split(mod[bp:bp + bs], 6, axis=-1)]
        s0_p = jnp.zeros((bp, nh, dk, dv), F32)
        cache0_p = jnp.zeros((bp, CONV_WIDTH - 1, cache_conv.shape[-1]), F32)
        xp, s_p, cb_p, tabs = _layer(xp, mods_p, s0_p, cache0_p, wts, rec_shape=(bp, tp), tt=1024)
        xs, s_s, cb_s, _ = _layer(xs, mods_s, state_gla[l], cache_conv[l], wts,
                                  rec_shape=(bs, ts), tt=bs * ts, experts_bf=tabs)
        for o, v in zip(outs, (s_p, cb_p, s_s, cb_s)):
            o.append(v)
    return (xp, xs.reshape(bs, ts, d), jnp.stack(outs[0]), jnp.stack(outs[1]),
            jnp.stack(outs[2]), jnp.stack(outs[3]))
```

```python
import functools

import jax
import jax.numpy as jnp
from jax import lax
from jax.experimental import pallas as pl
from jax.experimental.pallas import tpu as pltpu

F32 = jnp.float32
BF16 = jnp.bfloat16

CHUNK = 64
GLA_HEADS = 4
GLA_GATE_RANK = 16
GLA_GATE_NORM = 16.0
CONV_WIDTH = 31
PEER_HEADS = 8
PEER_NKEYS = 128
PEER_TOPK = 16
EPS = 1e-6

LANES = 128
SUBLANES = 8
MXU_WIDTH = 256
CONV_HALO = 32
VMEM_LIMIT = 52 * 1024 * 1024


def _params(*sem):
    return pltpu.CompilerParams(dimension_semantics=sem, vmem_limit_bytes=VMEM_LIMIT)


def _const_spec(shape):
    nd = len(shape)
    return pl.BlockSpec(shape, lambda *_: (0,) * nd, pipeline_mode=pl.Buffered(1))


def _rms(x):
    return x * lax.rsqrt(jnp.mean(x * x, axis=-1, keepdims=True) + EPS)


def _sigmoid(x):
    return 1.0 / (1.0 + jnp.exp(-x))


def _silu(x):
    return x * _sigmoid(x)


def _ada_kernel(c_ref, w_ref, b_ref, o_ref):
    c = c_ref[...]
    o_ref[...] = jnp.dot(_silu(c).astype(BF16), w_ref[...].astype(BF16),
                         preferred_element_type=F32) + b_ref[...]


def _ada(c, w_ada, b_ada):
    g, d = c.shape
    n = w_ada.shape[1]
    tc = 1024
    return pl.pallas_call(
        _ada_kernel,
        grid=(n // tc,),
        in_specs=[pl.BlockSpec((g, d), lambda j: (0, 0)),
                  pl.BlockSpec((d, tc), lambda j: (0, j)),
                  pl.BlockSpec((1, tc), lambda j: (0, j))],
        out_specs=pl.BlockSpec((g, tc), lambda j: (0, j)),
        out_shape=jax.ShapeDtypeStruct((g, n), F32),
        compiler_params=_params("arbitrary"),
        name="ada",
    )(c, w_ada, b_ada.reshape(1, n))


def _inproj_kernel(x_ref, sc_ref, sh_ref, g_ref, w_ref, wlr_ref, z_ref, zlr_ref, h_scr):
    @pl.when(pl.program_id(2) == 0)
    def _():
        h = _rms(x_ref[0]) * g_ref[...]
        h = (h * (1.0 + sc_ref[0]) + sh_ref[0]).astype(BF16)
        h_scr[...] = h
        zlr_ref[0] = lax.dot_general(h, wlr_ref[...], (((1,), (1,)), ((), ())),
                                     preferred_element_type=F32)

    z_ref[0] = lax.dot_general(h_scr[...], w_ref[...].astype(BF16), (((1,), (1,)), ((), ())),
                               preferred_element_type=F32)


def _mod_spec(m, tt):
    d = m.shape[-1]
    if m.shape[1] == 1:
        return pl.BlockSpec((1, 1, d), lambda b, i, *_: (b, 0, 0))
    return pl.BlockSpec((1, tt, d), lambda b, i, *_: (b, i, 0))


def _inproj(x, sc, sh, g, w_rows, w_lr, lr0, tt):
    bsz, t, d = x.shape
    n = w_rows.shape[0] - GLA_GATE_RANK
    tc = 1024
    assert lr0 % tc == 0 and n == 2 * lr0
    wrow = lambda b, i, j: (
        pl.multiple_of(j * tc + (j // (lr0 // tc)) * GLA_GATE_RANK, GLA_GATE_RANK), 0)
    return pl.pallas_call(
        _inproj_kernel,
        grid=(bsz, t // tt, n // tc),
        in_specs=[pl.BlockSpec((1, tt, d), lambda b, i, j: (b, i, 0)),
                  _mod_spec(sc, tt), _mod_spec(sh, tt),
                  pl.BlockSpec((1, d), lambda b, i, j: (0, 0)),
                  pl.BlockSpec((pl.Element(tc), pl.Element(d)), wrow),
                  pl.BlockSpec((LANES, d), lambda b, i, j: (0, 0))],
        out_specs=[pl.BlockSpec((1, tt, tc), lambda b, i, j: (b, i, j)),
                   pl.BlockSpec((1, tt, LANES), lambda b, i, j: (b, i, 0))],
        out_shape=[jax.ShapeDtypeStruct((bsz, t, n), F32),
                   jax.ShapeDtypeStruct((bsz, t, LANES), F32)],
        scratch_shapes=[pltpu.VMEM((tt, d), BF16)],
        compiler_params=_params("parallel", "parallel", "arbitrary"),
        name="inproj",
    )(x, sc, sh, g, w_rows, w_lr)


def _gla_kernel(q_ref, k_ref, v_ref, r_ref, lr_ref, wau_ref, ba_ref, gg_ref, s0_ref,
                o_ref, s_ref, st_scr, *, dk, dv):
    c = pl.program_id(1)
    streams, rows = q_ref.shape[0], q_ref.shape[1]

    @pl.when(c == 0)
    def _():
        for s in range(streams):
            for h in range(GLA_HEADS):
                st_scr[s, h] = s0_ref[s, h].T

    for s in range(streams):
        _gla_chunk(s, q_ref, k_ref, v_ref, r_ref, lr_ref, wau_ref, ba_ref, gg_ref, o_ref, st_scr,
                   dk=dk, dv=dv)

    @pl.when(c == pl.num_programs(1) - 1)
    def _():
        for s in range(streams):
            for h in range(GLA_HEADS):
                s_ref[s, h] = st_scr[s, h].T


def _gla_chunk(s, q_ref, k_ref, v_ref, r_ref, lr_ref, wau_ref, ba_ref, gg_ref, o_ref, st_scr,
               *, dk, dv):
    rows = q_ref.shape[1]
    pre = jnp.dot(lr_ref[s].astype(BF16), wau_ref[...], preferred_element_type=F32) + ba_ref[...]
    log_a = (jnp.minimum(pre, 0.0) - jnp.log1p(jnp.exp(-jnp.abs(pre)))) * (1.0 / GLA_GATE_NORM)
    row = lax.broadcasted_iota(jnp.int32, log_a.shape, 0)
    b = log_a
    shift = 1
    while shift < rows:
        b = b + jnp.where(row >= shift, pltpu.roll(b, shift, axis=0), 0.0)
        shift *= 2
    b_last = b[rows - 1:rows, :]
    e_pos = jnp.exp(b)
    q_in = q_ref[s] * (dk ** -0.5) * e_pos
    kk = k_ref[s]
    k_in = kk * jnp.exp(-b)
    k_out = kk * jnp.exp(b_last - b)
    e_last = jnp.exp(b_last)
    causal = (lax.broadcasted_iota(jnp.int32, (rows, rows), 0)
              >= lax.broadcasted_iota(jnp.int32, (rows, rows), 1))
    gg = gg_ref[...]
    for h in range(GLA_HEADS):
        ks = slice(h * dk, (h + 1) * dk)
        vs = slice(h * dv, (h + 1) * dv)
        qh = q_in[:, ks].astype(BF16)
        vh = v_ref[s, :, vs].astype(BF16)
        st = st_scr[s, h]
        att = lax.dot_general(qh, k_in[:, ks].astype(BF16), (((1,), (1,)), ((), ())),
                              preferred_element_type=F32)
        att = jnp.where(causal, att, 0.0).astype(BF16)
        o = jnp.dot(att, vh, preferred_element_type=F32)
        o = o + lax.dot_general(qh, st.astype(BF16), (((1,), (1,)), ((), ())),
                                preferred_element_type=F32)
        st_scr[s, h] = st * e_last[:, ks] + lax.dot_general(
            vh, k_out[:, ks].astype(BF16), (((0,), (0,)), ((), ())), preferred_element_type=F32)
        o = _rms(o) * gg * _silu(r_ref[s, :, vs])
        o_ref[s, :, vs] = o.astype(o_ref.dtype)


def _gla(z, zlr, w_au, b_alpha, g_gla_out, s0):
    bsz, t, _ = z.shape
    _, nh, dk, dv = s0.shape
    rows = min(CHUNK, t)
    sb = 2 if bsz % 2 == 0 else 1
    kern = functools.partial(_gla_kernel, dk=dk, dv=dv)
    return pl.pallas_call(
        kern,
        grid=(bsz // sb, t // rows),
        in_specs=[pl.BlockSpec((sb, rows, nh * dk), lambda b, c: (b, c, 0)),
                  pl.BlockSpec((sb, rows, nh * dk), lambda b, c: (b, c, 1)),
                  pl.BlockSpec((sb, rows, nh * dv), lambda b, c: (b, c, 1)),
                  pl.BlockSpec((sb, rows, nh * dv), lambda b, c: (b, c, 2)),
                  pl.BlockSpec((sb, rows, LANES), lambda b, c: (b, c, 0)),
                  pl.BlockSpec((LANES, nh * dk), lambda b, c: (0, 0)),
                  pl.BlockSpec((1, nh * dk), lambda b, c: (0, 0)),
                  pl.BlockSpec((1, dv), lambda b, c: (0, 0)),
                  pl.BlockSpec((sb, nh, dk, dv), lambda b, c: (b, 0, 0, 0))],
        out_specs=[pl.BlockSpec((sb, rows, nh * dv), lambda b, c: (b, c, 0)),
                   pl.BlockSpec((sb, nh, dk, dv), lambda b, c: (b, 0, 0, 0))],
        out_shape=[jax.ShapeDtypeStruct((bsz, t, nh * dv), BF16),
                   jax.ShapeDtypeStruct((bsz, nh, dk, dv), F32)],
        scratch_shapes=[pltpu.VMEM((sb, nh, dv, dk), F32)],
        compiler_params=_params("parallel", "arbitrary"),
        name="gla",
    )(z, z, z, z, zlr, w_au, b_alpha, g_gla_out, s0)


def _conv_kernel(a_ref, g_ref, cache_ref, wdw_ref, bdw_ref, gln_ref, bln_ref,
                 y_ref, newc_ref, shift_scr, conv_scr):
    i = pl.program_id(1)
    tt = a_ref.shape[1]
    hist = CONV_WIDTH - 1
    pad = CONV_HALO - hist
    win_scr = shift_scr.at[0]

    @pl.when(i == 0)
    def _():
        win_scr[0:pad, :] = jnp.zeros((pad, win_scr.shape[1]), F32)
        win_scr[pad:CONV_HALO, :] = cache_ref[0]

    win_scr[CONV_HALO:CONV_HALO + tt, :] = a_ref[0] * _sigmoid(g_ref[0])
    full = win_scr[...]
    for s in range(1, SUBLANES):
        shift_scr[s] = pltpu.roll(full, full.shape[0] - s, axis=0)
    bdw = bdw_ref[...]
    gln = gln_ref[...]
    bln = bln_ref[...]

    rows = min(tt, 4 * SUBLANES)

    def chunk(ci, carry):
        r0 = pl.multiple_of(ci * rows, rows)
        acc = jnp.zeros((rows // SUBLANES, SUBLANES, win_scr.shape[1]), F32)
        for w in range(CONV_WIDTH):
            q, s = divmod(pad + w, SUBLANES)
            x = shift_scr[s, pl.ds(r0 + q * SUBLANES, rows), :]
            acc = acc + x.reshape(rows // SUBLANES, SUBLANES, x.shape[1]) * wdw_ref[w]
        conv_scr[pl.ds(r0, rows), :] = acc.reshape(rows, win_scr.shape[1])
        return carry

    lax.fori_loop(0, tt // rows, chunk, 0)
    acc = conv_scr[...] + bdw
    mu = jnp.mean(acc, axis=-1, keepdims=True)
    xc = acc - mu
    yn = xc * lax.rsqrt(jnp.mean(xc * xc, axis=-1, keepdims=True) + EPS) * gln + bln
    y_ref[0] = _silu(yn).astype(y_ref.dtype)

    @pl.when(i == pl.num_programs(1) - 1)
    def _():
        newc_ref[0] = win_scr[tt + pad:tt + CONV_HALO, :]

    tail = win_scr[tt:tt + CONV_HALO, :]
    win_scr[0:CONV_HALO, :] = tail


def _conv(z, cache, w_dw, b_dw, g_ln, b_ln, tt):
    bsz, t, _ = z.shape
    _, hist, ch = cache.shape
    return pl.pallas_call(
        _conv_kernel,
        grid=(bsz, t // tt),
        in_specs=[pl.BlockSpec((1, tt, ch), lambda b, i: (b, i, 6)),
                  pl.BlockSpec((1, tt, ch), lambda b, i: (b, i, 7)),
                  pl.BlockSpec((1, hist, ch), lambda b, i: (b, 0, 0)),
                  pl.BlockSpec((CONV_WIDTH, SUBLANES, ch), lambda b, i: (0, 0, 0)),
                  pl.BlockSpec((1, ch), lambda b, i: (0, 0)),
                  pl.BlockSpec((1, ch), lambda b, i: (0, 0)),
                  pl.BlockSpec((1, ch), lambda b, i: (0, 0))],
        out_specs=[pl.BlockSpec((1, tt, ch), lambda b, i: (b, i, 0)),
                   pl.BlockSpec((1, hist, ch), lambda b, i: (b, 0, 0))],
        out_shape=[jax.ShapeDtypeStruct((bsz, t, ch), BF16),
                   jax.ShapeDtypeStruct((bsz, hist, ch), F32)],
        scratch_shapes=[pltpu.VMEM((SUBLANES, tt + CONV_HALO, ch), F32), pltpu.VMEM((tt, ch), F32)],
        compiler_params=_params("parallel", "arbitrary"),
        name="conv",
    )(z, z, cache, w_dw, b_dw, g_ln, b_ln)


def _merge_kernel(oa_ref, cb_ref, ga_ref, gb_ref, x_ref, gt1_ref, sc2_ref, sh2_ref,
                  gpost_ref, gpre_ref, wgo_ref, wco_ref, wout_ref, x1_ref, h2_ref):
    ya = jnp.dot(oa_ref[0], wgo_ref[...], preferred_element_type=F32)
    yb = jnp.dot(cb_ref[0], wco_ref[...], preferred_element_type=F32)
    m = _sigmoid(ga_ref[0]) * ya + _sigmoid(gb_ref[0]) * yb
    mix = jnp.dot(m.astype(BF16), wout_ref[...], preferred_element_type=F32)
    x1 = x_ref[0] + gt1_ref[0] * (_rms(mix) * gpost_ref[...])
    x1_ref[0] = x1
    h2 = _rms(x1) * gpre_ref[...]
    h2_ref[0] = (h2 * (1.0 + sc2_ref[0]) + sh2_ref[0]).astype(h2_ref.dtype)


def _merge(oa, cb, z, x, gt1, sc2, sh2, g_post, g_pre, w_go, w_co, w_out, tt):
    bsz, t, d = x.shape
    ch = cb.shape[-1]
    tok = lambda w: pl.BlockSpec((1, tt, w), lambda b, i: (b, i, 0))
    return pl.pallas_call(
        _merge_kernel,
        grid=(bsz, t // tt),
        in_specs=[tok(d), tok(ch),
                  pl.BlockSpec((1, tt, d), lambda b, i: (b, i, 4)),
                  pl.BlockSpec((1, tt, d), lambda b, i: (b, i, 5)),
                  tok(d), _mod_spec(gt1, tt), _mod_spec(sc2, tt), _mod_spec(sh2, tt),
                  _const_spec((1, d)), _const_spec((1, d)),
                  _const_spec(w_go.shape), _const_spec(w_co.shape), _const_spec(w_out.shape)],
        out_specs=[tok(d), tok(d)],
        out_shape=[jax.ShapeDtypeStruct((bsz, t, d), F32),
                   jax.ShapeDtypeStruct((bsz, t, d), BF16)],
        compiler_params=_params("parallel", "parallel"),
        name="merge",
    )(oa, cb, z, z, x, gt1, sc2, sh2, g_post, g_pre, w_go, w_co, w_out)


def _keyfold_kernel(ks_ref, wq_ref, o_ref):
    o_ref[...] = lax.dot_general(ks_ref[0], wq_ref[...], (((1,), (1,)), ((), ())),
                                 preferred_element_type=F32,
                                 precision=lax.Precision.HIGHEST).astype(o_ref.dtype)


def _keyfold(k_sub, w_pq):
    d = w_pq.shape[0]
    _, nk, dq = k_sub.shape
    nblk = w_pq.shape[1] // dq
    return pl.pallas_call(
        _keyfold_kernel,
        grid=(nblk,),
        in_specs=[pl.BlockSpec((1, nk, dq), lambda j: (j % 2, 0, 0)),
                  pl.BlockSpec((d, dq), lambda j: (0, j))],
        out_specs=pl.BlockSpec((nk, d), lambda j: (j, 0)),
        out_shape=jax.ShapeDtypeStruct((nblk * nk, d), BF16),
        compiler_params=_params("arbitrary"),
        name="keyfold",
    )(k_sub, w_pq)


def _top16(s):
    rows, n = s.shape
    rid = lax.broadcasted_iota(jnp.int32, s.shape, 0).astype(F32)
    sub = lax.broadcasted_iota(jnp.int32, (SUBLANES, n), 0).astype(F32)
    vals, idxs = [], []
    for _ in range(PEER_TOPK):
        nodes = [(s[g:g + SUBLANES], sub + float(g)) for g in range(0, rows, SUBLANES)]
        while len(nodes) > 1:
            merged = []
            for (va, ra), (vb, rb) in zip(nodes[0::2], nodes[1::2]):
                left = va >= vb
                merged.append((jnp.where(left, va, vb), jnp.where(left, ra, rb)))
            nodes = merged
        v8, r8 = nodes[0]
        m = jnp.max(v8, axis=0, keepdims=True)
        am = jnp.min(jnp.where(v8 == m, r8, float(rows)), axis=0, keepdims=True)
        vals.append(m)
        idxs.append(am)
        s = jnp.where(rid == am, -jnp.inf, s)
    return vals, idxs


def _stack_rows(rows_list, n):
    k = len(rows_list)
    rid = lax.broadcasted_iota(jnp.int32, (k, n), 0)
    out = jnp.zeros((k, n), F32)
    for r, v in enumerate(rows_list):
        out = jnp.where(rid == r, v, out)
    return out


def _topk_columns(s1, s2):
    tn = s1.shape[1]
    nk = PEER_NKEYS
    v1, i1 = _top16(s1)
    v2, i2 = _top16(s2)
    v2a = _stack_rows(v2, tn)
    i2a = _stack_rows(i2, tn)
    v1a = _stack_rows(v1, tn)
    i1a = _stack_rows(i1, tn)
    sub = lax.broadcasted_iota(jnp.int32, (SUBLANES, tn), 0)
    cand, cidx = [], []
    for a in range(SUBLANES):
        lim = PEER_TOPK // (a + 1)
        for b0 in range(0, lim, SUBLANES):
            cv = v1[a] + v2a[b0:b0 + SUBLANES]
            ci = i1[a] * float(nk) + i2a[b0:b0 + SUBLANES]
            if lim - b0 < SUBLANES:
                cv = jnp.where(sub < lim - b0, cv, -jnp.inf)
            cand.append(cv)
            cidx.append(ci)
    cand.append(v1a[SUBLANES:] + v2[0])
    cidx.append(i1a[SUBLANES:] * float(nk) + i2[0])
    cand = jnp.concatenate(cand, axis=0)
    cidx = jnp.concatenate(cidx, axis=0)
    rid = lax.broadcasted_iota(jnp.int32, cand.shape, 0).astype(F32)
    sc, ids = [], []
    for _ in range(PEER_TOPK):
        m = jnp.max(cand, axis=0, keepdims=True)
        am = jnp.min(jnp.where(cand == m, rid, float(cand.shape[0])), axis=0, keepdims=True)
        hit = rid == am
        sc.append(m)
        ids.append(jnp.sum(jnp.where(hit, cidx, 0.0), axis=0, keepdims=True))
        cand = jnp.where(hit, -jnp.inf, cand)
    sca = _stack_rows(sc, tn)
    ida = _stack_rows(ids, tn)
    e = jnp.exp(sca - sc[0])
    wts = e / jnp.sum(e, axis=0, keepdims=True)
    ia = jnp.floor(ida * (1.0 / nk))
    return wts, ia, ida - ia * float(nk)


def _topk_kernel(h_ref, tab_ref, w_ref, a_ref, b_ref, s_scr, w_scr, a_scr, b_scr):
    hd = pl.program_id(1)
    tn = h_ref.shape[0]
    nk = PEER_NKEYS
    s_scr[...] = lax.dot_general(tab_ref[...], h_ref[...], (((1,), (1,)), ((), ())),
                                 preferred_element_type=F32)
    r0 = pl.multiple_of(hd * PEER_TOPK, PEER_TOPK)

    width = min(tn, 2 * LANES)

    def lane_block(c, carry):
        cols = pl.ds(pl.multiple_of(c * width, width), width)
        wts, ia, ib = _topk_columns(s_scr[0:nk, cols], s_scr[nk:2 * nk, cols])
        w_scr[pl.ds(r0, PEER_TOPK), cols] = wts
        a_scr[pl.ds(r0, PEER_TOPK), cols] = ia
        b_scr[pl.ds(r0, PEER_TOPK), cols] = ib
        return carry

    lax.fori_loop(0, tn // width, lane_block, 0)

    @pl.when(hd == pl.num_programs(1) - 1)
    def _():
        w_ref[...] = w_scr[...].T
        a_ref[...] = a_scr[...].T
        b_ref[...] = b_scr[...].T


def _topk(h2, table, tn):
    n, d = h2.shape
    nsel = PEER_HEADS * PEER_TOPK
    out = jax.ShapeDtypeStruct((n, nsel), F32)
    ospec = pl.BlockSpec((tn, nsel), lambda i, h: (i, 0))
    return pl.pallas_call(
        _topk_kernel,
        grid=(n // tn, PEER_HEADS),
        in_specs=[pl.BlockSpec((tn, d), lambda i, h: (i, 0)),
                  pl.BlockSpec((2 * PEER_NKEYS, d), lambda i, h: (h, 0))],
        out_specs=[ospec, ospec, ospec],
        out_shape=[out, out, out],
        scratch_shapes=[pltpu.VMEM((2 * PEER_NKEYS, tn), F32)] + [pltpu.VMEM((nsel, tn), F32)] * 3,
        compiler_params=_params("parallel", "arbitrary"),
        name="topk",
    )(h2, table)


def _act_kernel(h_ref, u_ref, w_ref, a_ref, b_ref, coef_ref, sel_scr, s_scr):
    k = pl.program_id(1)
    nb = u_ref.shape[0] // LANES

    @pl.when(k == 0)
    def _():
        sel_scr[...] = jnp.zeros(sel_scr.shape, F32)
        s_scr[...] = jnp.zeros(s_scr.shape, F32)

    ia = a_ref[...]
    ib = b_ref[...].astype(jnp.int32)

    def gather(key0, sel):
        for j in range(nb):
            g = jnp.take_along_axis(s_scr[:, j * LANES:(j + 1) * LANES], ib, axis=1)
            sel = jnp.where(ia == key0 + float(j), g, sel)
        return sel

    base = (k * nb).astype(F32)
    sel = gather(base - float(nb), sel_scr[...])
    s_scr[...] = lax.dot_general(h_ref[...], u_ref[...].astype(BF16), (((1,), (1,)), ((), ())),
                                 preferred_element_type=F32)
    sel_scr[...] = sel

    @pl.when(k == pl.num_programs(1) - 1)
    def _():
        x = gather(base, sel)
        act = 0.5 * x * (1.0 + lax.erf(x * (2.0 ** -0.5)))
        coef_ref[...] = w_ref[...] * act


def _expert_act(h2, u_bf, wts, ia, ib, tn, te):
    n, d = h2.shape
    ne = u_bf.shape[0]
    nsel = wts.shape[1]
    sspec = pl.BlockSpec((tn, nsel), lambda i, k: (i, 0))
    return pl.pallas_call(
        _act_kernel,
        grid=(n // tn, ne // te),
        in_specs=[pl.BlockSpec((tn, d), lambda i, k: (i, 0)),
                  pl.BlockSpec((te, d), lambda i, k: (k, 0)),
                  sspec, sspec, sspec],
        out_specs=sspec,
        out_shape=jax.ShapeDtypeStruct((n, nsel), F32),
        scratch_shapes=[pltpu.VMEM((tn, nsel), F32), pltpu.VMEM((tn, te), F32)],
        compiler_params=_params("parallel", "arbitrary"),
        name="expert_act",
    )(h2, u_bf, wts, ia, ib)


def _mixmat_kernel(*refs, cast_values):
    if cast_values:
        coef_ref, a_ref, b_ref, v_ref, p_ref, vb_ref, p_scr = refs
        vb_ref[...] = v_ref[...].astype(vb_ref.dtype)
    else:
        coef_ref, a_ref, b_ref, p_ref, p_scr = refs
    tn = coef_ref.shape[0]
    nk = PEER_NKEYS
    kid = lax.broadcasted_iota(jnp.int32, (nk, coef_ref.shape[1]), 0).astype(F32).astype(BF16)
    one = jnp.ones((nk, coef_ref.shape[1]), BF16)
    zero = jnp.zeros((nk, coef_ref.shape[1]), BF16)

    unroll = tn

    def group(g, carry):
        t0 = pl.multiple_of(g * unroll, unroll)
        for s in range(unroll):
            c = coef_ref[pl.ds(t0 + s, 1), :].astype(BF16)
            ia = a_ref[pl.ds(t0 + s, 1), :].astype(BF16)
            ib = b_ref[pl.ds(t0 + s, 1), :].astype(BF16)
            lhs = jnp.where(ia == kid, c, zero)
            rhs = jnp.where(ib == kid, one, zero)
            row0 = (g * (unroll // SUBLANES) + s // SUBLANES) * (nk * SUBLANES) + s % SUBLANES
            p_scr[pl.ds(row0, nk, stride=SUBLANES), :] = lax.dot_general(
                lhs, rhs, (((1,), (1,)), ((), ())), preferred_element_type=F32)
        return carry

    lax.fori_loop(0, tn // unroll, group, 0)

    def emit(rr, carry):
        lo = pl.multiple_of(rr * (2 * nk * SUBLANES), SUBLANES)
        hi = lo + nk * SUBLANES
        row = pl.multiple_of(rr * 2 * SUBLANES, 2 * SUBLANES)
        for j in range(nk):
            tile = jnp.concatenate([p_scr[pl.ds(lo + j * SUBLANES, SUBLANES), :],
                                    p_scr[pl.ds(hi + j * SUBLANES, SUBLANES), :]], axis=0)
            p_ref[pl.ds(row, 2 * SUBLANES), j * nk:(j + 1) * nk] = tile.astype(p_ref.dtype)
        return carry

    lax.fori_loop(0, tn // (2 * SUBLANES), emit, 0)


def _mixmat(coef, ia, ib, tn, v_exp=None):
    n, nsel = coef.shape
    ne = PEER_NKEYS * PEER_NKEYS
    steps = n // tn
    sspec = pl.BlockSpec((tn, nsel), lambda i: (i, 0))
    in_specs, out_specs = [sspec, sspec, sspec], [pl.BlockSpec((tn, ne), lambda i: (i, 0))]
    out_shape, args = [jax.ShapeDtypeStruct((n, ne), BF16)], [coef, ia, ib]
    if v_exp is not None:
        d = v_exp.shape[1]
        vspec = pl.BlockSpec((ne // steps, d), lambda i: (i, 0))
        in_specs.append(vspec)
        out_specs.append(vspec)
        out_shape.append(jax.ShapeDtypeStruct((ne, d), BF16))
        args.append(v_exp)
    out = pl.pallas_call(
        functools.partial(_mixmat_kernel, cast_values=v_exp is not None),
        grid=(steps,),
        in_specs=in_specs,
        out_specs=out_specs,
        out_shape=out_shape,
        scratch_shapes=[pltpu.VMEM((tn * PEER_NKEYS, PEER_NKEYS), F32)],
        compiler_params=_params("parallel"),
        name="mixmat",
    )(*args)
    return out if v_exp is not None else (out[0], None)


def _value_kernel(p_ref, v_ref, x1_ref, gt2_ref, g_ref, y_ref, acc_scr):
    k = pl.program_id(2)

    @pl.when(k == 0)
    def _():
        acc_scr[...] = jnp.zeros(acc_scr.shape, F32)

    acc_scr[...] += jnp.dot(p_ref[0], v_ref[...], preferred_element_type=F32)

    @pl.when(k == pl.num_programs(2) - 1)
    def _():
        y_ref[0] = x1_ref[0] + gt2_ref[0] * (_rms(acc_scr[...]) * g_ref[...])


def _value(p, v_bf, x1, gt2, g_post, tt, tk):
    bsz, t, d = x1.shape
    ne = v_bf.shape[0]
    return pl.pallas_call(
        _value_kernel,
        grid=(bsz, t // tt, ne // tk),
        in_specs=[pl.BlockSpec((1, tt, tk), lambda b, i, k: (b, i, k)),
                  pl.BlockSpec((tk, d), lambda b, i, k: (k, 0)),
                  pl.BlockSpec((1, tt, d), lambda b, i, k: (b, i, 0),
                               pipeline_mode=pl.Buffered(1)),
                  _mod_spec(gt2, tt),
                  pl.BlockSpec((1, d), lambda b, i, k: (0, 0))],
        out_specs=pl.BlockSpec((1, tt, d), lambda b, i, k: (b, i, 0),
                               pipeline_mode=pl.Buffered(1)),
        out_shape=jax.ShapeDtypeStruct((bsz, t, d), F32),
        scratch_shapes=[pltpu.VMEM((tt, d), F32)],
        compiler_params=_params("parallel", "parallel", "arbitrary"),
        name="value",
    )(p, v_bf, x1, gt2, g_post)


def _layer(x, mods, s0, cache, wts, *, rec_shape, tt, v_bf=None):
    sh1, sc1, gt1, sh2, sc2, gt2 = mods
    bsz, t, d = x.shape
    z, zlr = _inproj(x, sc1, sh1, wts["g_pre_mix"], wts["w_rows"], wts["w_lr"], wts["lr0"], tt)
    rs, rt = rec_shape
    zr = z.reshape(rs, rt, z.shape[-1])
    oa, s_new = _gla(zr, zlr.reshape(rs, rt, LANES), wts["w_au"], wts["b_alpha"],
                     wts["g_gla_out"], s0)
    cb, c_new = _conv(zr, cache, wts["w_dw"], wts["b_dw"], wts["g_conv_ln"], wts["b_conv_ln"],
                      min(rt, 256))
    x1, h2 = _merge(oa.reshape(bsz, t, -1), cb.reshape(bsz, t, -1), z, x, gt1, sc2, sh2,
                    wts["g_post_mix"], wts["g_pre_ffn"], wts["w_gla_o"], wts["w_conv_o"],
                    wts["w_out"], min(t, 256))
    n = bsz * t
    h2f = h2.reshape(n, d)
    pw, ia, ib = _topk(h2f, wts["key_table"], min(n, 1024))
    coef = _expert_act(h2f, wts["u_exp"], pw, ia, ib, min(n, 1024), 1024)
    p, v_new = _mixmat(coef, ia, ib, 128, None if v_bf is not None else wts["v_exp"])
    v_bf = v_bf if v_bf is not None else v_new
    y = _value(p.reshape(bsz, t, -1), v_bf, x1, gt2, wts["g_post_ffn"], min(t, 512), 2048)
    return y, s_new, c_new, v_bf


def kernel(x_prompt, x_sample, state_gla, cache_conv, c_prompt, c_sample, w_ada, b_ada, g_pre_mix,
           g_post_mix, w_in, w_alpha_up, b_alpha, g_gla_out, w_gla_o, w_dw, b_dw, g_conv_ln,
           b_conv_ln, w_conv_o, w_out, g_pre_ffn, g_post_ffn, w_pq, k_sub1, k_sub2, u_exp, v_exp):
    depth = w_ada.shape[0]
    bp, tp, d = x_prompt.shape
    bs, ts, _ = x_sample.shape
    nh, dk, dv = state_gla.shape[2:]
    xp, xs = x_prompt, x_sample.reshape(1, bs * ts, d)
    outs = [[], [], [], []]
    for l in range(depth):
        lr0 = 2 * nh * dk + 2 * nh * dv
        wl = w_in[l].T
        row = lambda a: a[l].reshape(1, -1)
        wts = dict(
            w_rows=wl, lr0=lr0,
            w_lr=jnp.pad(wl[lr0:lr0 + GLA_GATE_RANK],
                         ((0, LANES - GLA_GATE_RANK), (0, 0))).astype(BF16),
            w_au=jnp.pad(w_alpha_up[l], ((0, LANES - GLA_GATE_RANK), (0, 0))).astype(BF16),
            b_alpha=row(b_alpha), g_gla_out=row(g_gla_out), g_pre_mix=row(g_pre_mix),
            g_post_mix=row(g_post_mix), g_pre_ffn=row(g_pre_ffn), g_post_ffn=row(g_post_ffn),
            w_dw=jnp.broadcast_to(w_dw[l][:, None, :], (CONV_WIDTH, SUBLANES, w_dw.shape[-1])),
            b_dw=row(b_dw), g_conv_ln=row(g_conv_ln), b_conv_ln=row(b_conv_ln),
            w_gla_o=w_gla_o[l].astype(BF16), w_conv_o=w_conv_o[l].astype(BF16),
            w_out=w_out[l].astype(BF16),
            key_table=_keyfold(jnp.stack([k_sub1[l], k_sub2[l]]), w_pq[l]),
            u_exp=u_exp[l], v_exp=v_exp[l],
        )
        c_all = jnp.concatenate(
            [c_prompt, c_sample, jnp.zeros((-(bp + bs) % SUBLANES, d), F32)], axis=0)
        mod = _ada(c_all, w_ada[l], b_ada[l])
        mods_p = [m.reshape(bp, 1, d) for m in jnp.split(mod[:bp], 6, axis=-1)]
        mods_s = [jnp.repeat(m, ts, axis=0).reshape(1, bs * ts, d)
                  for m in jnp.split(mod[bp:bp + bs], 6, axis=-1)]
        s0_p = jnp.zeros((bp, nh, dk, dv), F32)
        cache0_p = jnp.zeros((bp, CONV_WIDTH - 1, cache_conv.shape[-1]), F32)
        xp, s_p, cb_p, v_bf = _layer(xp, mods_p, s0_p, cache0_p, wts, rec_shape=(bp, tp), tt=1024)
        xs, s_s, cb_s, _ = _layer(xs, mods_s, state_gla[l], cache_conv[l], wts,
                                  rec_shape=(bs, ts), tt=bs * ts, v_bf=v_bf)
        for o, v in zip(outs, (s_p, cb_p, s_s, cb_s)):
            o.append(v)
    return (xp, xs.reshape(bs, ts, d), jnp.stack(outs[0]), jnp.stack(outs[1]),
            jnp.stack(outs[2]), jnp.stack(outs[3]))
```

```python
import functools

import jax
import jax.numpy as jnp
from jax import lax
from jax.experimental import pallas as pl
from jax.experimental.pallas import tpu as pltpu

F32 = jnp.float32
BF16 = jnp.bfloat16

CHUNK = 64
GLA_HEADS = 4
GLA_GATE_RANK = 16
GLA_GATE_NORM = 16.0
CONV_WIDTH = 31
PEER_HEADS = 8
PEER_NKEYS = 128
PEER_TOPK = 16
EPS = 1e-6

LANES = 128
SUBLANES = 8
MXU_WIDTH = 256
CONV_HALO = 32
VMEM_LIMIT = 52 * 1024 * 1024


def _params(*sem):
    return pltpu.CompilerParams(dimension_semantics=sem, vmem_limit_bytes=VMEM_LIMIT)


def _const_spec(shape):
    nd = len(shape)
    return pl.BlockSpec(shape, lambda *_: (0,) * nd, pipeline_mode=pl.Buffered(1))


def _rms(x):
    return x * lax.rsqrt(jnp.mean(x * x, axis=-1, keepdims=True) + EPS)


def _sigmoid(x):
    return 1.0 / (1.0 + jnp.exp(-x))


def _silu(x):
    return x * _sigmoid(x)


def _ada_kernel(c_ref, w_ref, b_ref, o_ref):
    c = c_ref[...]
    o_ref[...] = jnp.dot(_silu(c).astype(BF16), w_ref[...].astype(BF16),
                         preferred_element_type=F32) + b_ref[...]


def _ada(c, w_ada, b_ada):
    g, d = c.shape
    n = w_ada.shape[1]
    tc = 1024
    return pl.pallas_call(
        _ada_kernel,
        grid=(n // tc,),
        in_specs=[pl.BlockSpec((g, d), lambda j: (0, 0)),
                  pl.BlockSpec((d, tc), lambda j: (0, j)),
                  pl.BlockSpec((1, tc), lambda j: (0, j))],
        out_specs=pl.BlockSpec((g, tc), lambda j: (0, j)),
        out_shape=jax.ShapeDtypeStruct((g, n), F32),
        compiler_params=_params("arbitrary"),
        name="ada",
    )(c, w_ada, b_ada.reshape(1, n))


def _inproj_kernel(x_ref, sc_ref, sh_ref, g_ref, w_ref, wlr_ref, z_ref, zlr_ref, h_scr):
    @pl.when(pl.program_id(2) == 0)
    def _():
        h = _rms(x_ref[0]) * g_ref[...]
        h = (h * (1.0 + sc_ref[0]) + sh_ref[0]).astype(BF16)
        h_scr[...] = h
        zlr_ref[0] = lax.dot_general(h, wlr_ref[...], (((1,), (1,)), ((), ())),
                                     preferred_element_type=F32)

    z_ref[0] = lax.dot_general(h_scr[...], w_ref[...].astype(BF16), (((1,), (1,)), ((), ())),
                               preferred_element_type=F32)


def _mod_spec(m, tt):
    d = m.shape[-1]
    if m.shape[1] == 1:
        return pl.BlockSpec((1, 1, d), lambda b, i, *_: (b, 0, 0))
    return pl.BlockSpec((1, tt, d), lambda b, i, *_: (b, i, 0))


def _inproj(x, sc, sh, g, w_rows, w_lr, lr0, tt):
    bsz, t, d = x.shape
    n = w_rows.shape[0] - GLA_GATE_RANK
    tc = 1024
    assert lr0 % tc == 0 and n == 2 * lr0
    wrow = lambda b, i, j: (
        pl.multiple_of(j * tc + (j // (lr0 // tc)) * GLA_GATE_RANK, GLA_GATE_RANK), 0)
    return pl.pallas_call(
        _inproj_kernel,
        grid=(bsz, t // tt, n // tc),
        in_specs=[pl.BlockSpec((1, tt, d), lambda b, i, j: (b, i, 0)),
                  _mod_spec(sc, tt), _mod_spec(sh, tt),
                  pl.BlockSpec((1, d), lambda b, i, j: (0, 0)),
                  pl.BlockSpec((pl.Element(tc), pl.Element(d)), wrow),
                  pl.BlockSpec((LANES, d), lambda b, i, j: (0, 0))],
        out_specs=[pl.BlockSpec((1, tt, tc), lambda b, i, j: (b, i, j)),
                   pl.BlockSpec((1, tt, LANES), lambda b, i, j: (b, i, 0))],
        out_shape=[jax.ShapeDtypeStruct((bsz, t, n), F32),
                   jax.ShapeDtypeStruct((bsz, t, LANES), F32)],
        scratch_shapes=[pltpu.VMEM((tt, d), BF16)],
        compiler_params=_params("parallel", "parallel", "arbitrary"),
        name="inproj",
    )(x, sc, sh, g, w_rows, w_lr)


def _gla_kernel(q_ref, k_ref, v_ref, r_ref, lr_ref, wau_ref, ba_ref, gg_ref, s0_ref,
                o_ref, s_ref, st_scr, *, dk, dv):
    c = pl.program_id(1)
    streams, rows = q_ref.shape[0], q_ref.shape[1]

    @pl.when(c == 0)
    def _():
        for s in range(streams):
            for h in range(GLA_HEADS):
                st_scr[s, h] = s0_ref[s, h].T

    for s in range(streams):
        _gla_chunk(s, q_ref, k_ref, v_ref, r_ref, lr_ref, wau_ref, ba_ref, gg_ref, o_ref, st_scr,
                   dk=dk, dv=dv)

    @pl.when(c == pl.num_programs(1) - 1)
    def _():
        for s in range(streams):
            for h in range(GLA_HEADS):
                s_ref[s, h] = st_scr[s, h].T


def _gla_chunk(s, q_ref, k_ref, v_ref, r_ref, lr_ref, wau_ref, ba_ref, gg_ref, o_ref, st_scr,
               *, dk, dv):
    rows = q_ref.shape[1]
    pre = jnp.dot(lr_ref[s].astype(BF16), wau_ref[...], preferred_element_type=F32) + ba_ref[...]
    log_a = (jnp.minimum(pre, 0.0) - jnp.log1p(jnp.exp(-jnp.abs(pre)))) * (1.0 / GLA_GATE_NORM)
    row = lax.broadcasted_iota(jnp.int32, log_a.shape, 0)
    b = log_a
    shift = 1
    while shift < rows:
        b = b + jnp.where(row >= shift, pltpu.roll(b, shift, axis=0), 0.0)
        shift *= 2
    b_last = b[rows - 1:rows, :]
    e_pos = jnp.exp(b)
    q_in = q_ref[s] * (dk ** -0.5) * e_pos
    kk = k_ref[s]
    k_in = kk * jnp.exp(-b)
    k_out = kk * jnp.exp(b_last - b)
    e_last = jnp.exp(b_last)
    causal = (lax.broadcasted_iota(jnp.int32, (rows, rows), 0)
              >= lax.broadcasted_iota(jnp.int32, (rows, rows), 1))
    gg = gg_ref[...]
    for h in range(GLA_HEADS):
        ks = slice(h * dk, (h + 1) * dk)
        vs = slice(h * dv, (h + 1) * dv)
        qh = q_in[:, ks].astype(BF16)
        vh = v_ref[s, :, vs].astype(BF16)
        st = st_scr[s, h]
        att = lax.dot_general(qh, k_in[:, ks].astype(BF16), (((1,), (1,)), ((), ())),
                              preferred_element_type=F32)
        att = jnp.where(causal, att, 0.0).astype(BF16)
        o = jnp.dot(att, vh, preferred_element_type=F32)
        o = o + lax.dot_general(qh, st.astype(BF16), (((1,), (1,)), ((), ())),
                                preferred_element_type=F32)
        st_scr[s, h] = st * e_last[:, ks] + lax.dot_general(
            vh, k_out[:, ks].astype(BF16), (((0,), (0,)), ((), ())), preferred_element_type=F32)
        o = _rms(o) * gg * _silu(r_ref[s, :, vs])
        o_ref[s, :, vs] = o.astype(o_ref.dtype)


def _gla(z, zlr, w_au, b_alpha, g_gla_out, s0):
    bsz, t, _ = z.shape
    _, nh, dk, dv = s0.shape
    rows = min(CHUNK, t)
    sb = 2 if bsz % 2 == 0 else 1
    kern = functools.partial(_gla_kernel, dk=dk, dv=dv)
    return pl.pallas_call(
        kern,
        grid=(bsz // sb, t // rows),
        in_specs=[pl.BlockSpec((sb, rows, nh * dk), lambda b, c: (b, c, 0)),
                  pl.BlockSpec((sb, rows, nh * dk), lambda b, c: (b, c, 1)),
                  pl.BlockSpec((sb, rows, nh * dv), lambda b, c: (b, c, 1)),
                  pl.BlockSpec((sb, rows, nh * dv), lambda b, c: (b, c, 2)),
                  pl.BlockSpec((sb, rows, LANES), lambda b, c: (b, c, 0)),
                  pl.BlockSpec((LANES, nh * dk), lambda b, c: (0, 0)),
                  pl.BlockSpec((1, nh * dk), lambda b, c: (0, 0)),
                  pl.BlockSpec((1, dv), lambda b, c: (0, 0)),
                  pl.BlockSpec((sb, nh, dk, dv), lambda b, c: (b, 0, 0, 0))],
        out_specs=[pl.BlockSpec((sb, rows, nh * dv), lambda b, c: (b, c, 0)),
                   pl.BlockSpec((sb, nh, dk, dv), lambda b, c: (b, 0, 0, 0))],
        out_shape=[jax.ShapeDtypeStruct((bsz, t, nh * dv), BF16),
                   jax.ShapeDtypeStruct((bsz, nh, dk, dv), F32)],
        scratch_shapes=[pltpu.VMEM((sb, nh, dv, dk), F32)],
        compiler_params=_params("parallel", "arbitrary"),
        name="gla",
    )(z, z, z, z, zlr, w_au, b_alpha, g_gla_out, s0)


def _conv_kernel(a_ref, g_ref, cache_ref, wdw_ref, bdw_ref, gln_ref, bln_ref,
                 y_ref, newc_ref, shift_scr, conv_scr):
    i = pl.program_id(1)
    tt = a_ref.shape[1]
    hist = CONV_WIDTH - 1
    pad = CONV_HALO - hist
    win_scr = shift_scr.at[0]

    @pl.when(i == 0)
    def _():
        win_scr[0:pad, :] = jnp.zeros((pad, win_scr.shape[1]), F32)
        win_scr[pad:CONV_HALO, :] = cache_ref[0]

    win_scr[CONV_HALO:CONV_HALO + tt, :] = a_ref[0] * _sigmoid(g_ref[0])
    full = win_scr[...]
    for s in range(1, SUBLANES):
        shift_scr[s] = pltpu.roll(full, full.shape[0] - s, axis=0)
    bdw = bdw_ref[...]
    gln = gln_ref[...]
    bln = bln_ref[...]

    rows = min(tt, 4 * SUBLANES)

    def chunk(ci, carry):
        r0 = pl.multiple_of(ci * rows, rows)
        acc = jnp.zeros((rows // SUBLANES, SUBLANES, win_scr.shape[1]), F32)
        for w in range(CONV_WIDTH):
            q, s = divmod(pad + w, SUBLANES)
            x = shift_scr[s, pl.ds(r0 + q * SUBLANES, rows), :]
            acc = acc + x.reshape(rows // SUBLANES, SUBLANES, x.shape[1]) * wdw_ref[w]
        conv_scr[pl.ds(r0, rows), :] = acc.reshape(rows, win_scr.shape[1])
        return carry

    lax.fori_loop(0, tt // rows, chunk, 0)
    acc = conv_scr[...] + bdw
    mu = jnp.mean(acc, axis=-1, keepdims=True)
    xc = acc - mu
    yn = xc * lax.rsqrt(jnp.mean(xc * xc, axis=-1, keepdims=True) + EPS) * gln + bln
    y_ref[0] = _silu(yn).astype(y_ref.dtype)

    @pl.when(i == pl.num_programs(1) - 1)
    def _():
        newc_ref[0] = win_scr[tt + pad:tt + CONV_HALO, :]

    tail = win_scr[tt:tt + CONV_HALO, :]
    win_scr[0:CONV_HALO, :] = tail


def _conv(z, cache, w_dw, b_dw, g_ln, b_ln, tt):
    bsz, t, _ = z.shape
    _, hist, ch = cache.shape
    return pl.pallas_call(
        _conv_kernel,
        grid=(bsz, t // tt),
        in_specs=[pl.BlockSpec((1, tt, ch), lambda b, i: (b, i, 6)),
                  pl.BlockSpec((1, tt, ch), lambda b, i: (b, i, 7)),
                  pl.BlockSpec((1, hist, ch), lambda b, i: (b, 0, 0)),
                  pl.BlockSpec((CONV_WIDTH, SUBLANES, ch), lambda b, i: (0, 0, 0)),
                  pl.BlockSpec((1, ch), lambda b, i: (0, 0)),
                  pl.BlockSpec((1, ch), lambda b, i: (0, 0)),
                  pl.BlockSpec((1, ch), lambda b, i: (0, 0))],
        out_specs=[pl.BlockSpec((1, tt, ch), lambda b, i: (b, i, 0)),
                   pl.BlockSpec((1, hist, ch), lambda b, i: (b, 0, 0))],
        out_shape=[jax.ShapeDtypeStruct((bsz, t, ch), BF16),
                   jax.ShapeDtypeStruct((bsz, hist, ch), F32)],
        scratch_shapes=[pltpu.VMEM((SUBLANES, tt + CONV_HALO, ch), F32), pltpu.VMEM((tt, ch), F32)],
        compiler_params=_params("parallel", "arbitrary"),
        name="conv",
    )(z, z, cache, w_dw, b_dw, g_ln, b_ln)


def _merge_kernel(oa_ref, cb_ref, ga_ref, gb_ref, x_ref, gt1_ref, sc2_ref, sh2_ref,
                  gpost_ref, gpre_ref, wgo_ref, wco_ref, wout_ref, x1_ref, h2_ref):
    ya = jnp.dot(oa_ref[0], wgo_ref[...], preferred_element_type=F32)
    yb = jnp.dot(cb_ref[0], wco_ref[...], preferred_element_type=F32)
    m = _sigmoid(ga_ref[0]) * ya + _sigmoid(gb_ref[0]) * yb
    mix = jnp.dot(m.astype(BF16), wout_ref[...], preferred_element_type=F32)
    x1 = x_ref[0] + gt1_ref[0] * (_rms(mix) * gpost_ref[...])
    x1_ref[0] = x1
    h2 = _rms(x1) * gpre_ref[...]
    h2_ref[0] = (h2 * (1.0 + sc2_ref[0]) + sh2_ref[0]).astype(h2_ref.dtype)


def _merge(oa, cb, z, x, gt1, sc2, sh2, g_post, g_pre, w_go, w_co, w_out, tt):
    bsz, t, d = x.shape
    ch = cb.shape[-1]
    tok = lambda w: pl.BlockSpec((1, tt, w), lambda b, i: (b, i, 0))
    return pl.pallas_call(
        _merge_kernel,
        grid=(bsz, t // tt),
        in_specs=[tok(d), tok(ch),
                  pl.BlockSpec((1, tt, d), lambda b, i: (b, i, 4)),
                  pl.BlockSpec((1, tt, d), lambda b, i: (b, i, 5)),
                  tok(d), _mod_spec(gt1, tt), _mod_spec(sc2, tt), _mod_spec(sh2, tt),
                  _const_spec((1, d)), _const_spec((1, d)),
                  _const_spec(w_go.shape), _const_spec(w_co.shape), _const_spec(w_out.shape)],
        out_specs=[tok(d), tok(d)],
        out_shape=[jax.ShapeDtypeStruct((bsz, t, d), F32),
                   jax.ShapeDtypeStruct((bsz, t, d), BF16)],
        compiler_params=_params("parallel", "parallel"),
        name="merge",
    )(oa, cb, z, z, x, gt1, sc2, sh2, g_post, g_pre, w_go, w_co, w_out)


def _keyfold_kernel(ks_ref, wq_ref, o_ref):
    o_ref[...] = lax.dot_general(ks_ref[0], wq_ref[...], (((1,), (1,)), ((), ())),
                                 preferred_element_type=F32,
                                 precision=lax.Precision.HIGHEST).astype(o_ref.dtype)


def _keyfold(k_sub, w_pq):
    d = w_pq.shape[0]
    _, nk, dq = k_sub.shape
    nblk = w_pq.shape[1] // dq
    return pl.pallas_call(
        _keyfold_kernel,
        grid=(nblk,),
        in_specs=[pl.BlockSpec((1, nk, dq), lambda j: (j % 2, 0, 0)),
                  pl.BlockSpec((d, dq), lambda j: (0, j))],
        out_specs=pl.BlockSpec((nk, d), lambda j: (j, 0)),
        out_shape=jax.ShapeDtypeStruct((nblk * nk, d), BF16),
        compiler_params=_params("arbitrary"),
        name="keyfold",
    )(k_sub, w_pq)


def _top16(s):
    rows, n = s.shape
    rid = lax.broadcasted_iota(jnp.int32, s.shape, 0).astype(F32)
    sub = lax.broadcasted_iota(jnp.int32, (SUBLANES, n), 0).astype(F32)
    vals, idxs = [], []
    for _ in range(PEER_TOPK):
        nodes = [(s[g:g + SUBLANES], sub + float(g)) for g in range(0, rows, SUBLANES)]
        while len(nodes) > 1:
            merged = []
            for (va, ra), (vb, rb) in zip(nodes[0::2], nodes[1::2]):
                left = va >= vb
                merged.append((jnp.where(left, va, vb), jnp.where(left, ra, rb)))
            nodes = merged
        v8, r8 = nodes[0]
        m = jnp.max(v8, axis=0, keepdims=True)
        am = jnp.min(jnp.where(v8 == m, r8, float(rows)), axis=0, keepdims=True)
        vals.append(m)
        idxs.append(am)
        s = jnp.where(rid == am, -jnp.inf, s)
    return vals, idxs


def _stack_rows(rows_list, n):
    k = len(rows_list)
    rid = lax.broadcasted_iota(jnp.int32, (k, n), 0)
    out = jnp.zeros((k, n), F32)
    for r, v in enumerate(rows_list):
        out = jnp.where(rid == r, v, out)
    return out


def _topk_columns(s1, s2):
    tn = s1.shape[1]
    nk = PEER_NKEYS
    v1, i1 = _top16(s1)
    v2, i2 = _top16(s2)
    v2a = _stack_rows(v2, tn)
    i2a = _stack_rows(i2, tn)
    v1a = _stack_rows(v1, tn)
    i1a = _stack_rows(i1, tn)
    sub = lax.broadcasted_iota(jnp.int32, (SUBLANES, tn), 0)
    cand, cidx = [], []
    for a in range(SUBLANES):
        lim = PEER_TOPK // (a + 1)
        for b0 in range(0, lim, SUBLANES):
            cv = v1[a] + v2a[b0:b0 + SUBLANES]
            ci = i1[a] * float(nk) + i2a[b0:b0 + SUBLANES]
            if lim - b0 < SUBLANES:
                cv = jnp.where(sub < lim - b0, cv, -jnp.inf)
            cand.append(cv)
            cidx.append(ci)
    cand.append(v1a[SUBLANES:] + v2[0])
    cidx.append(i1a[SUBLANES:] * float(nk) + i2[0])
    cand = jnp.concatenate(cand, axis=0)
    cidx = jnp.concatenate(cidx, axis=0)
    rid = lax.broadcasted_iota(jnp.int32, cand.shape, 0).astype(F32)
    sc, ids = [], []
    for _ in range(PEER_TOPK):
        m = jnp.max(cand, axis=0, keepdims=True)
        am = jnp.min(jnp.where(cand == m, rid, float(cand.shape[0])), axis=0, keepdims=True)
        hit = rid == am
        sc.append(m)
        ids.append(jnp.sum(jnp.where(hit, cidx, 0.0), axis=0, keepdims=True))
        cand = jnp.where(hit, -jnp.inf, cand)
    sca = _stack_rows(sc, tn)
    ida = _stack_rows(ids, tn)
    e = jnp.exp(sca - sc[0])
    wts = e / jnp.sum(e, axis=0, keepdims=True)
    ia = jnp.floor(ida * (1.0 / nk))
    return wts, ia, ida - ia * float(nk)


def _topk_kernel(h_ref, tab_ref, w_ref, a_ref, b_ref, s_scr, w_scr, a_scr, b_scr):
    hd = pl.program_id(1)
    tn = h_ref.shape[0]
    nk = PEER_NKEYS
    s_scr[...] = lax.dot_general(tab_ref[...], h_ref[...], (((1,), (1,)), ((), ())),
                                 preferred_element_type=F32)
    r0 = pl.multiple_of(hd * PEER_TOPK, PEER_TOPK)

    width = min(tn, 2 * LANES)

    def lane_block(c, carry):
        cols = pl.ds(pl.multiple_of(c * width, width), width)
        wts, ia, ib = _topk_columns(s_scr[0:nk, cols], s_scr[nk:2 * nk, cols])
        w_scr[pl.ds(r0, PEER_TOPK), cols] = wts
        a_scr[pl.ds(r0, PEER_TOPK), cols] = ia
        b_scr[pl.ds(r0, PEER_TOPK), cols] = ib
        return carry

    lax.fori_loop(0, tn // width, lane_block, 0)

    @pl.when(hd == pl.num_programs(1) - 1)
    def _():
        w_ref[...] = w_scr[...].T
        a_ref[...] = a_scr[...].T
        b_ref[...] = b_scr[...].T


def _topk(h2, table, tn):
    n, d = h2.shape
    nsel = PEER_HEADS * PEER_TOPK
    out = jax.ShapeDtypeStruct((n, nsel), F32)
    ospec = pl.BlockSpec((tn, nsel), lambda i, h: (i, 0))
    return pl.pallas_call(
        _topk_kernel,
        grid=(n // tn, PEER_HEADS),
        in_specs=[pl.BlockSpec((tn, d), lambda i, h: (i, 0)),
                  pl.BlockSpec((2 * PEER_NKEYS, d), lambda i, h: (h, 0))],
        out_specs=[ospec, ospec, ospec],
        out_shape=[out, out, out],
        scratch_shapes=[pltpu.VMEM((2 * PEER_NKEYS, tn), F32)] + [pltpu.VMEM((nsel, tn), F32)] * 3,
        compiler_params=_params("parallel", "arbitrary"),
        name="topk",
    )(h2, table)


def _act_kernel(h_ref, u_ref, w_ref, a_ref, b_ref, coef_ref, sel_scr, s_scr):
    k = pl.program_id(1)
    nb = u_ref.shape[0] // LANES

    @pl.when(k == 0)
    def _():
        sel_scr[...] = jnp.zeros(sel_scr.shape, F32)
        s_scr[...] = jnp.zeros(s_scr.shape, F32)

    ia = a_ref[...]
    ib = b_ref[...].astype(jnp.int32)

    def gather(key0, sel):
        for j in range(nb):
            g = jnp.take_along_axis(s_scr[:, j * LANES:(j + 1) * LANES], ib, axis=1)
            sel = jnp.where(ia == key0 + float(j), g, sel)
        return sel

    base = (k * nb).astype(F32)
    sel = gather(base - float(nb), sel_scr[...])
    s_scr[...] = lax.dot_general(h_ref[...], u_ref[...].astype(BF16), (((1,), (1,)), ((), ())),
                                 preferred_element_type=F32)
    sel_scr[...] = sel

    @pl.when(k == pl.num_programs(1) - 1)
    def _():
        x = gather(base, sel)
        act = 0.5 * x * (1.0 + lax.erf(x * (2.0 ** -0.5)))
        coef_ref[...] = w_ref[...] * act


def _expert_act(h2, u_bf, wts, ia, ib, tn, te):
    n, d = h2.shape
    ne = u_bf.shape[0]
    nsel = wts.shape[1]
    sspec = pl.BlockSpec((tn, nsel), lambda i, k: (i, 0))
    return pl.pallas_call(
        _act_kernel,
        grid=(n // tn, ne // te),
        in_specs=[pl.BlockSpec((tn, d), lambda i, k: (i, 0)),
                  pl.BlockSpec((te, d), lambda i, k: (k, 0)),
                  sspec, sspec, sspec],
        out_specs=sspec,
        out_shape=jax.ShapeDtypeStruct((n, nsel), F32),
        scratch_shapes=[pltpu.VMEM((tn, nsel), F32), pltpu.VMEM((tn, te), F32)],
        compiler_params=_params("parallel", "arbitrary"),
        name="expert_act",
    )(h2, u_bf, wts, ia, ib)


def _mixmat_kernel(*refs, cast_values):
    if cast_values:
        coef_ref, a_ref, b_ref, v_ref, p_ref, vb_ref, p_scr = refs
        vb_ref[...] = v_ref[...].astype(vb_ref.dtype)
    else:
        coef_ref, a_ref, b_ref, p_ref, p_scr = refs
    tn = coef_ref.shape[0]
    nk = PEER_NKEYS
    kid = lax.broadcasted_iota(jnp.int32, (nk, coef_ref.shape[1]), 0).astype(F32).astype(BF16)
    one = jnp.ones((nk, coef_ref.shape[1]), BF16)
    zero = jnp.zeros((nk, coef_ref.shape[1]), BF16)

    pack = 2 * SUBLANES

    def emit(rr):
        lo = rr * pack * nk
        hi = lo + nk * SUBLANES
        for j in range(nk):
            tile = jnp.concatenate([p_scr[lo + j * SUBLANES:lo + (j + 1) * SUBLANES, :],
                                    p_scr[hi + j * SUBLANES:hi + (j + 1) * SUBLANES, :]], axis=0)
            p_ref[rr * pack:(rr + 1) * pack, j * nk:(j + 1) * nk] = tile.astype(p_ref.dtype)

    for s in range(tn):
        c = coef_ref[s:s + 1, :].astype(BF16)
        ia = a_ref[s:s + 1, :].astype(BF16)
        ib = b_ref[s:s + 1, :].astype(BF16)
        lhs = jnp.where(ia == kid, c, zero)
        rhs = jnp.where(ib == kid, one, zero)
        row0 = (s // SUBLANES) * (nk * SUBLANES) + s % SUBLANES
        p_scr[pl.ds(row0, nk, stride=SUBLANES), :] = lax.dot_general(
            lhs, rhs, (((1,), (1,)), ((), ())), preferred_element_type=F32)
        if s % pack == pack - 1:
            emit(s // pack)


def _mixmat(coef, ia, ib, tn, v_exp=None):
    n, nsel = coef.shape
    ne = PEER_NKEYS * PEER_NKEYS
    steps = n // tn
    sspec = pl.BlockSpec((tn, nsel), lambda i: (i, 0))
    in_specs, out_specs = [sspec, sspec, sspec], [pl.BlockSpec((tn, ne), lambda i: (i, 0))]
    out_shape, args = [jax.ShapeDtypeStruct((n, ne), BF16)], [coef, ia, ib]
    if v_exp is not None:
        d = v_exp.shape[1]
        vspec = pl.BlockSpec((ne // steps, d), lambda i: (i, 0))
        in_specs.append(vspec)
        out_specs.append(vspec)
        out_shape.append(jax.ShapeDtypeStruct((ne, d), BF16))
        args.append(v_exp)
    out = pl.pallas_call(
        functools.partial(_mixmat_kernel, cast_values=v_exp is not None),
        grid=(steps,),
        in_specs=in_specs,
        out_specs=out_specs,
        out_shape=out_shape,
        scratch_shapes=[pltpu.VMEM((tn * PEER_NKEYS, PEER_NKEYS), F32)],
        compiler_params=_params("parallel"),
        name="mixmat",
    )(*args)
    return out if v_exp is not None else (out[0], None)


def _value_kernel(p_ref, v_ref, x1_ref, gt2_ref, g_ref, y_ref, acc_scr):
    k = pl.program_id(2)

    @pl.when(k == 0)
    def _():
        acc_scr[...] = jnp.zeros(acc_scr.shape, F32)

    acc_scr[...] += jnp.dot(p_ref[0], v_ref[...], preferred_element_type=F32)

    @pl.when(k == pl.num_programs(2) - 1)
    def _():
        y_ref[0] = x1_ref[0] + gt2_ref[0] * (_rms(acc_scr[...]) * g_ref[...])


def _value(p, v_bf, x1, gt2, g_post, tt, tk):
    bsz, t, d = x1.shape
    ne = v_bf.shape[0]
    return pl.pallas_call(
        _value_kernel,
        grid=(bsz, t // tt, ne // tk),
        in_specs=[pl.BlockSpec((1, tt, tk), lambda b, i, k: (b, i, k)),
                  pl.BlockSpec((tk, d), lambda b, i, k: (k, 0)),
                  pl.BlockSpec((1, tt, d), lambda b, i, k: (b, i, 0),
                               pipeline_mode=pl.Buffered(1)),
                  _mod_spec(gt2, tt),
                  pl.BlockSpec((1, d), lambda b, i, k: (0, 0))],
        out_specs=pl.BlockSpec((1, tt, d), lambda b, i, k: (b, i, 0),
                               pipeline_mode=pl.Buffered(1)),
        out_shape=jax.ShapeDtypeStruct((bsz, t, d), F32),
        scratch_shapes=[pltpu.VMEM((tt, d), F32)],
        compiler_params=_params("parallel", "parallel", "arbitrary"),
        name="value",
    )(p, v_bf, x1, gt2, g_post)


def _layer(x, mods, s0, cache, wts, *, rec_shape, tt, v_bf=None):
    sh1, sc1, gt1, sh2, sc2, gt2 = mods
    bsz, t, d = x.shape
    z, zlr = _inproj(x, sc1, sh1, wts["g_pre_mix"], wts["w_rows"], wts["w_lr"], wts["lr0"], tt)
    rs, rt = rec_shape
    zr = z.reshape(rs, rt, z.shape[-1])
    oa, s_new = _gla(zr, zlr.reshape(rs, rt, LANES), wts["w_au"], wts["b_alpha"],
                     wts["g_gla_out"], s0)
    cb, c_new = _conv(zr, cache, wts["w_dw"], wts["b_dw"], wts["g_conv_ln"], wts["b_conv_ln"],
                      min(rt, 256))
    x1, h2 = _merge(oa.reshape(bsz, t, -1), cb.reshape(bsz, t, -1), z, x, gt1, sc2, sh2,
                    wts["g_post_mix"], wts["g_pre_ffn"], wts["w_gla_o"], wts["w_conv_o"],
                    wts["w_out"], min(t, 256))
    n = bsz * t
    h2f = h2.reshape(n, d)
    pw, ia, ib = _topk(h2f, wts["key_table"], min(n, 1024))
    coef = _expert_act(h2f, wts["u_exp"], pw, ia, ib, min(n, 1024), 1024)
    p, v_new = _mixmat(coef, ia, ib, 128, None if v_bf is not None else wts["v_exp"])
    v_bf = v_bf if v_bf is not None else v_new
    y = _value(p.reshape(bsz, t, -1), v_bf, x1, gt2, wts["g_post_ffn"], min(t, 512), 2048)
    return y, s_new, c_new, v_bf


def kernel(x_prompt, x_sample, state_gla, cache_conv, c_prompt, c_sample, w_ada, b_ada, g_pre_mix,
           g_post_mix, w_in, w_alpha_up, b_alpha, g_gla_out, w_gla_o, w_dw, b_dw, g_conv_ln,
           b_conv_ln, w_conv_o, w_out, g_pre_ffn, g_post_ffn, w_pq, k_sub1, k_sub2, u_exp, v_exp):
    depth = w_ada.shape[0]
    bp, tp, d = x_prompt.shape
    bs, ts, _ = x_sample.shape
    nh, dk, dv = state_gla.shape[2:]
    xp, xs = x_prompt, x_sample.reshape(1, bs * ts, d)
    outs = [[], [], [], []]
    for l in range(depth):
        lr0 = 2 * nh * dk + 2 * nh * dv
        wl = w_in[l].T
        row = lambda a: a[l].reshape(1, -1)
        wts = dict(
            w_rows=wl, lr0=lr0,
            w_lr=jnp.pad(wl[lr0:lr0 + GLA_GATE_RANK],
                         ((0, LANES - GLA_GATE_RANK), (0, 0))).astype(BF16),
            w_au=jnp.pad(w_alpha_up[l], ((0, LANES - GLA_GATE_RANK), (0, 0))).astype(BF16),
            b_alpha=row(b_alpha), g_gla_out=row(g_gla_out), g_pre_mix=row(g_pre_mix),
            g_post_mix=row(g_post_mix), g_pre_ffn=row(g_pre_ffn), g_post_ffn=row(g_post_ffn),
            w_dw=jnp.broadcast_to(w_dw[l][:, None, :], (CONV_WIDTH, SUBLANES, w_dw.shape[-1])),
            b_dw=row(b_dw), g_conv_ln=row(g_conv_ln), b_conv_ln=row(b_conv_ln),
            w_gla_o=w_gla_o[l].astype(BF16), w_conv_o=w_conv_o[l].astype(BF16),
            w_out=w_out[l].astype(BF16),
            key_table=_keyfold(jnp.stack([k_sub1[l], k_sub2[l]]), w_pq[l]),
            u_exp=u_exp[l], v_exp=v_exp[l],
        )
        c_all = jnp.concatenate(
            [c_prompt, c_sample, jnp.zeros((-(bp + bs) % SUBLANES, d), F32)], axis=0)
        mod = _ada(c_all, w_ada[l], b_ada[l])
        mods_p = [m.reshape(bp, 1, d) for m in jnp.split(mod[:bp], 6, axis=-1)]
        mods_s = [jnp.repeat(m, ts, axis=0).reshape(1, bs * ts, d)
                  for m in jnp.split(mod[bp:bp + bs], 6, axis=-1)]
        s0_p = jnp.zeros((bp, nh, dk, dv), F32)
        cache0_p = jnp.zeros((bp, CONV_WIDTH - 1, cache_conv.shape[-1]), F32)
        xp, s_p, cb_p, v_bf = _layer(xp, mods_p, s0_p, cache0_p, wts, rec_shape=(bp, tp), tt=1024)
        xs, s_s, cb_s, _ = _layer(xs, mods_s, state_gla[l], cache_conv[l], wts,
                                  rec_shape=(bs, ts), tt=bs * ts, v_bf=v_bf)
        for o, v in zip(outs, (s_p, cb_p, s_s, cb_s)):
            o.append(v)
    return (xp, xs.reshape(bs, ts, d), jnp.stack(outs[0]), jnp.stack(outs[1]),
            jnp.stack(outs[2]), jnp.stack(outs[3]))
```

```python
import functools

import jax
import jax.numpy as jnp
from jax import lax
from jax.experimental import pallas as pl
from jax.experimental.pallas import tpu as pltpu

F32 = jnp.float32
BF16 = jnp.bfloat16

CHUNK = 64
GLA_HEADS = 4
GLA_GATE_RANK = 16
GLA_GATE_NORM = 16.0
CONV_WIDTH = 31
PEER_HEADS = 8
PEER_NKEYS = 128
PEER_TOPK = 16
EPS = 1e-6

LANES = 128
SUBLANES = 8
MXU_WIDTH = 256
CONV_HALO = 32
VMEM_LIMIT = 52 * 1024 * 1024


def _params(*sem):
    return pltpu.CompilerParams(dimension_semantics=sem, vmem_limit_bytes=VMEM_LIMIT)


def _const_spec(shape):
    nd = len(shape)
    return pl.BlockSpec(shape, lambda *_: (0,) * nd, pipeline_mode=pl.Buffered(1))


def _rms(x):
    return x * lax.rsqrt(jnp.mean(x * x, axis=-1, keepdims=True) + EPS)


def _sigmoid(x):
    return 1.0 / (1.0 + jnp.exp(-x))


def _silu(x):
    return x * _sigmoid(x)


def _ada_kernel(c_ref, w_ref, b_ref, o_ref):
    c = c_ref[...]
    o_ref[...] = jnp.dot(_silu(c).astype(BF16), w_ref[...].astype(BF16),
                         preferred_element_type=F32) + b_ref[...]


def _ada(c, w_ada, b_ada):
    g, d = c.shape
    n = w_ada.shape[1]
    tc = 1024
    return pl.pallas_call(
        _ada_kernel,
        grid=(n // tc,),
        in_specs=[pl.BlockSpec((g, d), lambda j: (0, 0)),
                  pl.BlockSpec((d, tc), lambda j: (0, j)),
                  pl.BlockSpec((1, tc), lambda j: (0, j))],
        out_specs=pl.BlockSpec((g, tc), lambda j: (0, j)),
        out_shape=jax.ShapeDtypeStruct((g, n), F32),
        compiler_params=_params("arbitrary"),
        name="ada",
    )(c, w_ada, b_ada.reshape(1, n))


def _inproj_kernel(x_ref, sc_ref, sh_ref, g_ref, w_ref, wlr_ref, z_ref, zlr_ref, h_scr):
    @pl.when(pl.program_id(2) == 0)
    def _():
        h = _rms(x_ref[0]) * g_ref[...]
        h = (h * (1.0 + sc_ref[0]) + sh_ref[0]).astype(BF16)
        h_scr[...] = h
        zlr_ref[0] = lax.dot_general(h, wlr_ref[...], (((1,), (1,)), ((), ())),
                                     preferred_element_type=F32)

    z_ref[0] = lax.dot_general(h_scr[...], w_ref[...].astype(BF16), (((1,), (1,)), ((), ())),
                               preferred_element_type=F32)


def _mod_spec(m, tt):
    d = m.shape[-1]
    if m.shape[1] == 1:
        return pl.BlockSpec((1, 1, d), lambda b, i, *_: (b, 0, 0))
    return pl.BlockSpec((1, tt, d), lambda b, i, *_: (b, i, 0))


def _inproj(x, sc, sh, g, w_rows, w_lr, lr0, tt):
    bsz, t, d = x.shape
    n = w_rows.shape[0] - GLA_GATE_RANK
    tc = 1024
    assert lr0 % tc == 0 and n == 2 * lr0
    wrow = lambda b, i, j: (
        pl.multiple_of(j * tc + (j // (lr0 // tc)) * GLA_GATE_RANK, GLA_GATE_RANK), 0)
    return pl.pallas_call(
        _inproj_kernel,
        grid=(bsz, t // tt, n // tc),
        in_specs=[pl.BlockSpec((1, tt, d), lambda b, i, j: (b, i, 0)),
                  _mod_spec(sc, tt), _mod_spec(sh, tt),
                  pl.BlockSpec((1, d), lambda b, i, j: (0, 0)),
                  pl.BlockSpec((pl.Element(tc), pl.Element(d)), wrow),
                  pl.BlockSpec((LANES, d), lambda b, i, j: (0, 0))],
        out_specs=[pl.BlockSpec((1, tt, tc), lambda b, i, j: (b, i, j)),
                   pl.BlockSpec((1, tt, LANES), lambda b, i, j: (b, i, 0))],
        out_shape=[jax.ShapeDtypeStruct((bsz, t, n), F32),
                   jax.ShapeDtypeStruct((bsz, t, LANES), F32)],
        scratch_shapes=[pltpu.VMEM((tt, d), BF16)],
        compiler_params=_params("parallel", "parallel", "arbitrary"),
        name="inproj",
    )(x, sc, sh, g, w_rows, w_lr)


def _gla_kernel(q_ref, k_ref, v_ref, r_ref, lr_ref, wau_ref, ba_ref, gg_ref, s0_ref,
                o_ref, s_ref, st_scr, *, dk, dv):
    c = pl.program_id(1)
    streams, rows = q_ref.shape[0], q_ref.shape[1]

    @pl.when(c == 0)
    def _():
        for s in range(streams):
            for h in range(GLA_HEADS):
                st_scr[s, h] = s0_ref[s, h].T

    for s in range(streams):
        _gla_chunk(s, q_ref, k_ref, v_ref, r_ref, lr_ref, wau_ref, ba_ref, gg_ref, o_ref, st_scr,
                   dk=dk, dv=dv)

    @pl.when(c == pl.num_programs(1) - 1)
    def _():
        for s in range(streams):
            for h in range(GLA_HEADS):
                s_ref[s, h] = st_scr[s, h].T


def _gla_chunk(s, q_ref, k_ref, v_ref, r_ref, lr_ref, wau_ref, ba_ref, gg_ref, o_ref, st_scr,
               *, dk, dv):
    rows = q_ref.shape[1]
    pre = jnp.dot(lr_ref[s].astype(BF16), wau_ref[...], preferred_element_type=F32) + ba_ref[...]
    log_a = (jnp.minimum(pre, 0.0) - jnp.log1p(jnp.exp(-jnp.abs(pre)))) * (1.0 / GLA_GATE_NORM)
    row = lax.broadcasted_iota(jnp.int32, log_a.shape, 0)
    b = log_a
    shift = 1
    while shift < rows:
        b = b + jnp.where(row >= shift, pltpu.roll(b, shift, axis=0), 0.0)
        shift *= 2
    b_last = b[rows - 1:rows, :]
    e_pos = jnp.exp(b)
    q_in = q_ref[s] * (dk ** -0.5) * e_pos
    kk = k_ref[s]
    k_in = kk * jnp.exp(-b)
    k_out = kk * jnp.exp(b_last - b)
    e_last = jnp.exp(b_last)
    causal = (lax.broadcasted_iota(jnp.int32, (rows, rows), 0)
              >= lax.broadcasted_iota(jnp.int32, (rows, rows), 1))
    gg = gg_ref[...]
    for h in range(GLA_HEADS):
        ks = slice(h * dk, (h + 1) * dk)
        vs = slice(h * dv, (h + 1) * dv)
        qh = q_in[:, ks].astype(BF16)
        vh = v_ref[s, :, vs].astype(BF16)
        st = st_scr[s, h]
        att = lax.dot_general(qh, k_in[:, ks].astype(BF16), (((1,), (1,)), ((), ())),
                              preferred_element_type=F32)
        att = jnp.where(causal, att, 0.0).astype(BF16)
        o = jnp.dot(att, vh, preferred_element_type=F32)
        o = o + lax.dot_general(qh, st.astype(BF16), (((1,), (1,)), ((), ())),
                                preferred_element_type=F32)
        st_scr[s, h] = st * e_last[:, ks] + lax.dot_general(
            vh, k_out[:, ks].astype(BF16), (((0,), (0,)), ((), ())), preferred_element_type=F32)
        o = _rms(o) * gg * _silu(r_ref[s, :, vs])
        o_ref[s, :, vs] = o.astype(o_ref.dtype)


def _gla(z, zlr, w_au, b_alpha, g_gla_out, s0):
    bsz, t, _ = z.shape
    _, nh, dk, dv = s0.shape
    rows = min(CHUNK, t)
    sb = 2 if bsz % 2 == 0 else 1
    kern = functools.partial(_gla_kernel, dk=dk, dv=dv)
    return pl.pallas_call(
        kern,
        grid=(bsz // sb, t // rows),
        in_specs=[pl.BlockSpec((sb, rows, nh * dk), lambda b, c: (b, c, 0)),
                  pl.BlockSpec((sb, rows, nh * dk), lambda b, c: (b, c, 1)),
                  pl.BlockSpec((sb, rows, nh * dv), lambda b, c: (b, c, 1)),
                  pl.BlockSpec((sb, rows, nh * dv), lambda b, c: (b, c, 2)),
                  pl.BlockSpec((sb, rows, LANES), lambda b, c: (b, c, 0)),
                  pl.BlockSpec((LANES, nh * dk), lambda b, c: (0, 0)),
                  pl.BlockSpec((1, nh * dk), lambda b, c: (0, 0)),
                  pl.BlockSpec((1, dv), lambda b, c: (0, 0)),
                  pl.BlockSpec((sb, nh, dk, dv), lambda b, c: (b, 0, 0, 0))],
        out_specs=[pl.BlockSpec((sb, rows, nh * dv), lambda b, c: (b, c, 0)),
                   pl.BlockSpec((sb, nh, dk, dv), lambda b, c: (b, 0, 0, 0))],
        out_shape=[jax.ShapeDtypeStruct((bsz, t, nh * dv), BF16),
                   jax.ShapeDtypeStruct((bsz, nh, dk, dv), F32)],
        scratch_shapes=[pltpu.VMEM((sb, nh, dv, dk), F32)],
        compiler_params=_params("parallel", "arbitrary"),
        name="gla",
    )(z, z, z, z, zlr, w_au, b_alpha, g_gla_out, s0)


def _conv_kernel(a_ref, g_ref, cache_ref, wdw_ref, bdw_ref, gln_ref, bln_ref,
                 y_ref, newc_ref, shift_scr, conv_scr):
    i = pl.program_id(1)
    tt = a_ref.shape[1]
    hist = CONV_WIDTH - 1
    pad = CONV_HALO - hist
    win_scr = shift_scr.at[0]

    @pl.when(i == 0)
    def _():
        win_scr[0:pad, :] = jnp.zeros((pad, win_scr.shape[1]), F32)
        win_scr[pad:CONV_HALO, :] = cache_ref[0]

    win_scr[CONV_HALO:CONV_HALO + tt, :] = a_ref[0] * _sigmoid(g_ref[0])
    full = win_scr[...]
    for s in range(1, SUBLANES):
        shift_scr[s] = pltpu.roll(full, full.shape[0] - s, axis=0)
    bdw = bdw_ref[...]
    gln = gln_ref[...]
    bln = bln_ref[...]

    rows = min(tt, 4 * SUBLANES)

    def chunk(ci, carry):
        r0 = pl.multiple_of(ci * rows, rows)
        acc = jnp.zeros((rows // SUBLANES, SUBLANES, win_scr.shape[1]), F32)
        for w in range(CONV_WIDTH):
            q, s = divmod(pad + w, SUBLANES)
            x = shift_scr[s, pl.ds(r0 + q * SUBLANES, rows), :]
            acc = acc + x.reshape(rows // SUBLANES, SUBLANES, x.shape[1]) * wdw_ref[w]
        conv_scr[pl.ds(r0, rows), :] = acc.reshape(rows, win_scr.shape[1])
        return carry

    lax.fori_loop(0, tt // rows, chunk, 0)
    acc = conv_scr[...] + bdw
    mu = jnp.mean(acc, axis=-1, keepdims=True)
    xc = acc - mu
    yn = xc * lax.rsqrt(jnp.mean(xc * xc, axis=-1, keepdims=True) + EPS) * gln + bln
    y_ref[0] = _silu(yn).astype(y_ref.dtype)

    @pl.when(i == pl.num_programs(1) - 1)
    def _():
        newc_ref[0] = win_scr[tt + pad:tt + CONV_HALO, :]

    tail = win_scr[tt:tt + CONV_HALO, :]
    win_scr[0:CONV_HALO, :] = tail


def _conv(z, cache, w_dw, b_dw, g_ln, b_ln, tt):
    bsz, t, _ = z.shape
    _, hist, ch = cache.shape
    return pl.pallas_call(
        _conv_kernel,
        grid=(bsz, t // tt),
        in_specs=[pl.BlockSpec((1, tt, ch), lambda b, i: (b, i, 6)),
                  pl.BlockSpec((1, tt, ch), lambda b, i: (b, i, 7)),
                  pl.BlockSpec((1, hist, ch), lambda b, i: (b, 0, 0)),
                  pl.BlockSpec((CONV_WIDTH, SUBLANES, ch), lambda b, i: (0, 0, 0)),
                  pl.BlockSpec((1, ch), lambda b, i: (0, 0)),
                  pl.BlockSpec((1, ch), lambda b, i: (0, 0)),
                  pl.BlockSpec((1, ch), lambda b, i: (0, 0))],
        out_specs=[pl.BlockSpec((1, tt, ch), lambda b, i: (b, i, 0)),
                   pl.BlockSpec((1, hist, ch), lambda b, i: (b, 0, 0))],
        out_shape=[jax.ShapeDtypeStruct((bsz, t, ch), BF16),
                   jax.ShapeDtypeStruct((bsz, hist, ch), F32)],
        scratch_shapes=[pltpu.VMEM((SUBLANES, tt + CONV_HALO, ch), F32), pltpu.VMEM((tt, ch), F32)],
        compiler_params=_params("parallel", "arbitrary"),
        name="conv",
    )(z, z, cache, w_dw, b_dw, g_ln, b_ln)


def _merge_kernel(oa_ref, cb_ref, ga_ref, gb_ref, x_ref, gt1_ref, sc2_ref, sh2_ref,
                  gpost_ref, gpre_ref, wgo_ref, wco_ref, wout_ref, x1_ref, h2_ref):
    ya = jnp.dot(oa_ref[0], wgo_ref[...], preferred_element_type=F32)
    yb = jnp.dot(cb_ref[0], wco_ref[...], preferred_element_type=F32)
    m = _sigmoid(ga_ref[0]) * ya + _sigmoid(gb_ref[0]) * yb
    mix = jnp.dot(m.astype(BF16), wout_ref[...], preferred_element_type=F32)
    x1 = x_ref[0] + gt1_ref[0] * (_rms(mix) * gpost_ref[...])
    x1_ref[0] = x1
    h2 = _rms(x1) * gpre_ref[...]
    h2_ref[0] = (h2 * (1.0 + sc2_ref[0]) + sh2_ref[0]).astype(h2_ref.dtype)


def _merge(oa, cb, z, x, gt1, sc2, sh2, g_post, g_pre, w_go, w_co, w_out, tt):
    bsz, t, d = x.shape
    ch = cb.shape[-1]
    tok = lambda w: pl.BlockSpec((1, tt, w), lambda b, i: (b, i, 0))
    return pl.pallas_call(
        _merge_kernel,
        grid=(bsz, t // tt),
        in_specs=[tok(d), tok(ch),
                  pl.BlockSpec((1, tt, d), lambda b, i: (b, i, 4)),
                  pl.BlockSpec((1, tt, d), lambda b, i: (b, i, 5)),
                  tok(d), _mod_spec(gt1, tt), _mod_spec(sc2, tt), _mod_spec(sh2, tt),
                  _const_spec((1, d)), _const_spec((1, d)),
                  _const_spec(w_go.shape), _const_spec(w_co.shape), _const_spec(w_out.shape)],
        out_specs=[tok(d), tok(d)],
        out_shape=[jax.ShapeDtypeStruct((bsz, t, d), F32),
                   jax.ShapeDtypeStruct((bsz, t, d), BF16)],
        compiler_params=_params("parallel", "parallel"),
        name="merge",
    )(oa, cb, z, z, x, gt1, sc2, sh2, g_post, g_pre, w_go, w_co, w_out)


def _keyfold_kernel(ks_ref, wq_ref, o_ref):
    o_ref[...] = lax.dot_general(ks_ref[0], wq_ref[...], (((1,), (1,)), ((), ())),
                                 preferred_element_type=F32,
                                 precision=lax.Precision.HIGHEST).astype(o_ref.dtype)


def _keyfold(k_sub, w_pq):
    d = w_pq.shape[0]
    _, nk, dq = k_sub.shape
    nblk = w_pq.shape[1] // dq
    return pl.pallas_call(
        _keyfold_kernel,
        grid=(nblk,),
        in_specs=[pl.BlockSpec((1, nk, dq), lambda j: (j % 2, 0, 0)),
                  pl.BlockSpec((d, dq), lambda j: (0, j))],
        out_specs=pl.BlockSpec((nk, d), lambda j: (j, 0)),
        out_shape=jax.ShapeDtypeStruct((nblk * nk, d), BF16),
        compiler_params=_params("arbitrary"),
        name="keyfold",
    )(k_sub, w_pq)


def _top16(s):
    rows, n = s.shape
    rid = lax.broadcasted_iota(jnp.int32, s.shape, 0).astype(F32)
    sub = lax.broadcasted_iota(jnp.int32, (SUBLANES, n), 0).astype(F32)
    vals, idxs = [], []
    for _ in range(PEER_TOPK):
        nodes = [(s[g:g + SUBLANES], sub + float(g)) for g in range(0, rows, SUBLANES)]
        while len(nodes) > 1:
            merged = []
            for (va, ra), (vb, rb) in zip(nodes[0::2], nodes[1::2]):
                left = va >= vb
                merged.append((jnp.where(left, va, vb), jnp.where(left, ra, rb)))
            nodes = merged
        v8, r8 = nodes[0]
        m = jnp.max(v8, axis=0, keepdims=True)
        am = jnp.min(jnp.where(v8 == m, r8, float(rows)), axis=0, keepdims=True)
        vals.append(m)
        idxs.append(am)
        s = jnp.where(rid == am, -jnp.inf, s)
    return vals, idxs


def _stack_rows(rows_list, n):
    k = len(rows_list)
    rid = lax.broadcasted_iota(jnp.int32, (k, n), 0)
    out = jnp.zeros((k, n), F32)
    for r, v in enumerate(rows_list):
        out = jnp.where(rid == r, v, out)
    return out


def _topk_columns(s1, s2):
    tn = s1.shape[1]
    nk = PEER_NKEYS
    v1, i1 = _top16(s1)
    v2, i2 = _top16(s2)
    v2a = _stack_rows(v2, tn)
    i2a = _stack_rows(i2, tn)
    v1a = _stack_rows(v1, tn)
    i1a = _stack_rows(i1, tn)
    sub = lax.broadcasted_iota(jnp.int32, (SUBLANES, tn), 0)
    cand, cidx = [], []
    for a in range(SUBLANES):
        lim = PEER_TOPK // (a + 1)
        for b0 in range(0, lim, SUBLANES):
            cv = v1[a] + v2a[b0:b0 + SUBLANES]
            ci = i1[a] * float(nk) + i2a[b0:b0 + SUBLANES]
            if lim - b0 < SUBLANES:
                cv = jnp.where(sub < lim - b0, cv, -jnp.inf)
            cand.append(cv)
            cidx.append(ci)
    cand.append(v1a[SUBLANES:] + v2[0])
    cidx.append(i1a[SUBLANES:] * float(nk) + i2[0])
    cand = jnp.concatenate(cand, axis=0)
    cidx = jnp.concatenate(cidx, axis=0)
    rid = lax.broadcasted_iota(jnp.int32, cand.shape, 0).astype(F32)
    sc, ids = [], []
    for _ in range(PEER_TOPK):
        m = jnp.max(cand, axis=0, keepdims=True)
        am = jnp.min(jnp.where(cand == m, rid, float(cand.shape[0])), axis=0, keepdims=True)
        hit = rid == am
        sc.append(m)
        ids.append(jnp.sum(jnp.where(hit, cidx, 0.0), axis=0, keepdims=True))
        cand = jnp.where(hit, -jnp.inf, cand)
    sca = _stack_rows(sc, tn)
    ida = _stack_rows(ids, tn)
    e = jnp.exp(sca - sc[0])
    wts = e / jnp.sum(e, axis=0, keepdims=True)
    ia = jnp.floor(ida * (1.0 / nk))
    return wts, ia, ida - ia * float(nk)


def _topk_kernel(h_ref, tab_ref, w_ref, a_ref, b_ref, s_scr, w_scr, a_scr, b_scr):
    hd = pl.program_id(1)
    tn = h_ref.shape[0]
    nk = PEER_NKEYS
    s_scr[...] = lax.dot_general(tab_ref[...], h_ref[...], (((1,), (1,)), ((), ())),
                                 preferred_element_type=F32)
    r0 = pl.multiple_of(hd * PEER_TOPK, PEER_TOPK)

    width = min(tn, 2 * LANES)

    def lane_block(c, carry):
        cols = pl.ds(pl.multiple_of(c * width, width), width)
        wts, ia, ib = _topk_columns(s_scr[0:nk, cols], s_scr[nk:2 * nk, cols])
        w_scr[pl.ds(r0, PEER_TOPK), cols] = wts
        a_scr[pl.ds(r0, PEER_TOPK), cols] = ia
        b_scr[pl.ds(r0, PEER_TOPK), cols] = ib
        return carry

    lax.fori_loop(0, tn // width, lane_block, 0)

    @pl.when(hd == pl.num_programs(1) - 1)
    def _():
        w_ref[...] = w_scr[...].T
        a_ref[...] = a_scr[...].T
        b_ref[...] = b_scr[...].T


def _topk(h2, table, tn):
    n, d = h2.shape
    nsel = PEER_HEADS * PEER_TOPK
    out = jax.ShapeDtypeStruct((n, nsel), F32)
    ospec = pl.BlockSpec((tn, nsel), lambda i, h: (i, 0))
    return pl.pallas_call(
        _topk_kernel,
        grid=(n // tn, PEER_HEADS),
        in_specs=[pl.BlockSpec((tn, d), lambda i, h: (i, 0)),
                  pl.BlockSpec((2 * PEER_NKEYS, d), lambda i, h: (h, 0))],
        out_specs=[ospec, ospec, ospec],
        out_shape=[out, out, out],
        scratch_shapes=[pltpu.VMEM((2 * PEER_NKEYS, tn), F32)] + [pltpu.VMEM((nsel, tn), F32)] * 3,
        compiler_params=_params("parallel", "arbitrary"),
        name="topk",
    )(h2, table)


def _act_kernel(h_ref, u_ref, w_ref, a_ref, b_ref, coef_ref, sel_scr, s_scr):
    k = pl.program_id(1)
    nb = u_ref.shape[0] // LANES

    @pl.when(k == 0)
    def _():
        sel_scr[...] = jnp.zeros(sel_scr.shape, F32)
        s_scr[...] = jnp.zeros(s_scr.shape, F32)

    ia = a_ref[...]
    ib = b_ref[...].astype(jnp.int32)

    def gather(key0, sel):
        for j in range(nb):
            g = jnp.take_along_axis(s_scr[:, j * LANES:(j + 1) * LANES], ib, axis=1)
            sel = jnp.where(ia == key0 + float(j), g, sel)
        return sel

    base = (k * nb).astype(F32)
    sel = gather(base - float(nb), sel_scr[...])
    s_scr[...] = lax.dot_general(h_ref[...], u_ref[...].astype(BF16), (((1,), (1,)), ((), ())),
                                 preferred_element_type=F32)
    sel_scr[...] = sel

    @pl.when(k == pl.num_programs(1) - 1)
    def _():
        x = gather(base, sel)
        act = 0.5 * x * (1.0 + lax.erf(x * (2.0 ** -0.5)))
        coef_ref[...] = w_ref[...] * act


def _expert_act(h2, u_bf, wts, ia, ib, tn, te):
    n, d = h2.shape
    ne = u_bf.shape[0]
    nsel = wts.shape[1]
    sspec = pl.BlockSpec((tn, nsel), lambda i, k: (i, 0))
    return pl.pallas_call(
        _act_kernel,
        grid=(n // tn, ne // te),
        in_specs=[pl.BlockSpec((tn, d), lambda i, k: (i, 0)),
                  pl.BlockSpec((te, d), lambda i, k: (k, 0)),
                  sspec, sspec, sspec],
        out_specs=sspec,
        out_shape=jax.ShapeDtypeStruct((n, nsel), F32),
        scratch_shapes=[pltpu.VMEM((tn, nsel), F32), pltpu.VMEM((tn, te), F32)],
        compiler_params=_params("parallel", "arbitrary"),
        name="expert_act",
    )(h2, u_bf, wts, ia, ib)


def _mixmat_kernel(*refs, cast_values):
    if cast_values:
        coef_ref, a_ref, b_ref, v_ref, p_ref, vb_ref, p_scr = refs
        vb_ref[...] = v_ref[...].astype(vb_ref.dtype)
    else:
        coef_ref, a_ref, b_ref, p_ref, p_scr = refs
    tn = coef_ref.shape[0]
    nk = PEER_NKEYS
    kid = lax.broadcasted_iota(jnp.int32, (nk, coef_ref.shape[1]), 0).astype(F32).astype(BF16)
    one = jnp.ones((nk, coef_ref.shape[1]), BF16)
    zero = jnp.zeros((nk, coef_ref.shape[1]), BF16)

    pack = 2 * SUBLANES

    def emit(rr):
        lo = rr * pack * nk
        hi = lo + nk * SUBLANES
        for j in range(nk):
            tile = jnp.concatenate([p_scr[lo + j * SUBLANES:lo + (j + 1) * SUBLANES, :],
                                    p_scr[hi + j * SUBLANES:hi + (j + 1) * SUBLANES, :]], axis=0)
            p_ref[rr * pack:(rr + 1) * pack, j * nk:(j + 1) * nk] = tile.astype(p_ref.dtype)

    for s in range(tn):
        c = coef_ref[s:s + 1, :].astype(BF16)
        ia = a_ref[s:s + 1, :].astype(BF16)
        ib = b_ref[s:s + 1, :].astype(BF16)
        lhs = jnp.where(ia == kid, c, zero)
        rhs = jnp.where(ib == kid, one, zero)
        row0 = (s // SUBLANES) * (nk * SUBLANES) + s % SUBLANES
        p_scr[pl.ds(row0, nk, stride=SUBLANES), :] = lax.dot_general(
            lhs, rhs, (((1,), (1,)), ((), ())), preferred_element_type=F32)
        if s % pack == pack - 1:
            emit(s // pack)


def _mixmat(coef, ia, ib, tn, v_exp=None):
    n, nsel = coef.shape
    ne = PEER_NKEYS * PEER_NKEYS
    steps = n // tn
    sspec = pl.BlockSpec((tn, nsel), lambda i: (i, 0))
    in_specs, out_specs = [sspec, sspec, sspec], [pl.BlockSpec((tn, ne), lambda i: (i, 0))]
    out_shape, args = [jax.ShapeDtypeStruct((n, ne), BF16)], [coef, ia, ib]
    if v_exp is not None:
        d = v_exp.shape[1]
        vspec = pl.BlockSpec((ne // steps, d), lambda i: (i, 0))
        in_specs.append(vspec)
        out_specs.append(vspec)
        out_shape.append(jax.ShapeDtypeStruct((ne, d), BF16))
        args.append(v_exp)
    out = pl.pallas_call(
        functools.partial(_mixmat_kernel, cast_values=v_exp is not None),
        grid=(steps,),
        in_specs=in_specs,
        out_specs=out_specs,
        out_shape=out_shape,
        scratch_shapes=[pltpu.VMEM((tn * PEER_NKEYS, PEER_NKEYS), F32)],
        compiler_params=_params("parallel"),
        name="mixmat",
    )(*args)
    return out if v_exp is not None else (out[0], None)


def _value_kernel(p_ref, v_ref, x1_ref, gt2_ref, g_ref, y_ref, acc_scr):
    k = pl.program_id(2)

    @pl.when(k == 0)
    def _():
        acc_scr[...] = jnp.zeros(acc_scr.shape, F32)

    acc_scr[...] += jnp.dot(p_ref[0], v_ref[...], preferred_element_type=F32)

    @pl.when(k == pl.num_programs(2) - 1)
    def _():
        y_ref[0] = x1_ref[0] + gt2_ref[0] * (_rms(acc_scr[...]) * g_ref[...])


def _value(p, v_bf, x1, gt2, g_post, tt, tk):
    bsz, t, d = x1.shape
    ne = v_bf.shape[0]
    return pl.pallas_call(
        _value_kernel,
        grid=(bsz, t // tt, ne // tk),
        in_specs=[pl.BlockSpec((1, tt, tk), lambda b, i, k: (b, i, k)),
                  pl.BlockSpec((tk, d), lambda b, i, k: (k, 0)),
                  pl.BlockSpec((1, tt, d), lambda b, i, k: (b, i, 0)),
                  _mod_spec(gt2, tt),
                  pl.BlockSpec((1, d), lambda b, i, k: (0, 0))],
        out_specs=pl.BlockSpec((1, tt, d), lambda b, i, k: (b, i, 0)),
        out_shape=jax.ShapeDtypeStruct((bsz, t, d), F32),
        scratch_shapes=[pltpu.VMEM((tt, d), F32)],
        compiler_params=_params("parallel", "parallel", "arbitrary"),
        name="value",
    )(p, v_bf, x1, gt2, g_post)


def _layer(x, mods, s0, cache, wts, *, rec_shape, tt, v_bf=None):
    sh1, sc1, gt1, sh2, sc2, gt2 = mods
    bsz, t, d = x.shape
    z, zlr = _inproj(x, sc1, sh1, wts["g_pre_mix"], wts["w_rows"], wts["w_lr"], wts["lr0"], tt)
    rs, rt = rec_shape
    zr = z.reshape(rs, rt, z.shape[-1])
    oa, s_new = _gla(zr, zlr.reshape(rs, rt, LANES), wts["w_au"], wts["b_alpha"],
                     wts["g_gla_out"], s0)
    cb, c_new = _conv(zr, cache, wts["w_dw"], wts["b_dw"], wts["g_conv_ln"], wts["b_conv_ln"],
                      min(rt, 256))
    x1, h2 = _merge(oa.reshape(bsz, t, -1), cb.reshape(bsz, t, -1), z, x, gt1, sc2, sh2,
                    wts["g_post_mix"], wts["g_pre_ffn"], wts["w_gla_o"], wts["w_conv_o"],
                    wts["w_out"], min(t, 256))
    n = bsz * t
    h2f = h2.reshape(n, d)
    pw, ia, ib = _topk(h2f, wts["key_table"], min(n, 1024))
    coef = _expert_act(h2f, wts["u_exp"], pw, ia, ib, min(n, 1024), 1024)
    p, v_new = _mixmat(coef, ia, ib, 128, None if v_bf is not None else wts["v_exp"])
    v_bf = v_bf if v_bf is not None else v_new
    y = _value(p.reshape(bsz, t, -1), v_bf, x1, gt2, wts["g_post_ffn"], min(t, 512), 2048)
    return y, s_new, c_new, v_bf


def kernel(x_prompt, x_sample, state_gla, cache_conv, c_prompt, c_sample, w_ada, b_ada, g_pre_mix,
           g_post_mix, w_in, w_alpha_up, b_alpha, g_gla_out, w_gla_o, w_dw, b_dw, g_conv_ln,
           b_conv_ln, w_conv_o, w_out, g_pre_ffn, g_post_ffn, w_pq, k_sub1, k_sub2, u_exp, v_exp):
    depth = w_ada.shape[0]
    bp, tp, d = x_prompt.shape
    bs, ts, _ = x_sample.shape
    nh, dk, dv = state_gla.shape[2:]
    xp, xs = x_prompt, x_sample.reshape(1, bs * ts, d)
    outs = [[], [], [], []]
    for l in range(depth):
        lr0 = 2 * nh * dk + 2 * nh * dv
        wl = w_in[l].T
        row = lambda a: a[l].reshape(1, -1)
        wts = dict(
            w_rows=wl, lr0=lr0,
            w_lr=jnp.pad(wl[lr0:lr0 + GLA_GATE_RANK],
                         ((0, LANES - GLA_GATE_RANK), (0, 0))).astype(BF16),
            w_au=jnp.pad(w_alpha_up[l], ((0, LANES - GLA_GATE_RANK), (0, 0))).astype(BF16),
            b_alpha=row(b_alpha), g_gla_out=row(g_gla_out), g_pre_mix=row(g_pre_mix),
            g_post_mix=row(g_post_mix), g_pre_ffn=row(g_pre_ffn), g_post_ffn=row(g_post_ffn),
            w_dw=jnp.broadcast_to(w_dw[l][:, None, :], (CONV_WIDTH, SUBLANES, w_dw.shape[-1])),
            b_dw=row(b_dw), g_conv_ln=row(g_conv_ln), b_conv_ln=row(b_conv_ln),
            w_gla_o=w_gla_o[l].astype(BF16), w_conv_o=w_conv_o[l].astype(BF16),
            w_out=w_out[l].astype(BF16),
            key_table=_keyfold(jnp.stack([k_sub1[l], k_sub2[l]]), w_pq[l]),
            u_exp=u_exp[l], v_exp=v_exp[l],
        )
        c_all = jnp.concatenate(
            [c_prompt, c_sample, jnp.zeros((-(bp + bs) % SUBLANES, d), F32)], axis=0)
        mod = _ada(c_all, w_ada[l], b_ada[l])
        mods_p = [m.reshape(bp, 1, d) for m in jnp.split(mod[:bp], 6, axis=-1)]
        mods_s = [jnp.repeat(m, ts, axis=0).reshape(1, bs * ts, d)
                  for m in jnp.split(mod[bp:bp + bs], 6, axis=-1)]
        s0_p = jnp.zeros((bp, nh, dk, dv), F32)
        cache0_p = jnp.zeros((bp, CONV_WIDTH - 1, cache_conv.shape[-1]), F32)
        xp, s_p, cb_p, v_bf = _layer(xp, mods_p, s0_p, cache0_p, wts, rec_shape=(bp, tp), tt=1024)
        xs, s_s, cb_s, _ = _layer(xs, mods_s, state_gla[l], cache_conv[l], wts,
                                  rec_shape=(bs, ts), tt=bs * ts, v_bf=v_bf)
        for o, v in zip(outs, (s_p, cb_p, s_s, cb_s)):
            o.append(v)
    return (xp, xs.reshape(bs, ts, d), jnp.stack(outs[0]), jnp.stack(outs[1]),
            jnp.stack(outs[2]), jnp.stack(outs[3]))
```
